```python
import math, functools
import jax, jax.numpy as jnp
from jax import lax
import numpy as np

D_MODEL = 1024
BATCH = 2
SEQ = 8192
DEPTH = 1
DEC_BATCH = 128
DEC_SEQ = 1
PAST_LEN = 8192
PAGE_SIZE = 128

DIFF_HEADS = 8
DIFF_HD = 64
DIFF_QK = DIFF_HEADS * 2 * DIFF_HD
DIFF_V = DIFF_HEADS * 2 * DIFF_HD
Q_BLOCK = 128
RET_HEADS = 4
RET_DK = D_MODEL // RET_HEADS
RET_DV = 2 * RET_DK
RET_QK = RET_HEADS * RET_DK
RET_V = RET_HEADS * RET_DV
RET_CHUNK = 128
ROPE_BASE = 10000.0
N_EXPERTS = 32
TOP_K = 4
D_FF = D_MODEL
MOE_BLOCK = 128
SWIGLU_LIMIT = 7.0
SWIGLU_ALPHA = 1.702
EPS = 1e-6
SUBLN_EPS = 1e-5
IN_WIDTH = 2 * DIFF_QK + DIFF_V + 2 * RET_QK + 2 * RET_V + 2 * D_MODEL

kernel_name = 'hybrid_diffattn_retention_moe_step'


def rms_norm(x, w=None, eps=EPS):
    xf = x.astype(jnp.float32)
    y = xf * lax.rsqrt(jnp.mean(xf * xf, axis=-1, keepdims=True) + eps)
    if w is not None:
        y = y * w.astype(jnp.float32)
    return y.astype(x.dtype)


def split_columns(z):
    sizes = (DIFF_QK, DIFF_QK, DIFF_V, RET_QK, RET_QK, RET_V, RET_V, D_MODEL, D_MODEL)
    idx = np.cumsum(np.array(sizes))[:-1].tolist()
    return jnp.split(z, idx, axis=-1)


def rotary(x, pos):
    d = x.shape[-1]
    half = d // 2
    inv = ROPE_BASE ** (-jnp.arange(half, dtype=jnp.float32) * 2.0 / d)
    ang = pos.astype(jnp.float32)[:, None] * inv[None, :]
    cos = jnp.cos(ang)[None, :, None, :]
    sin = jnp.sin(ang)[None, :, None, :]
    xf = x.astype(jnp.float32)
    x1, x2 = xf[..., :half], xf[..., half:]
    return jnp.concatenate([x1 * cos - x2 * sin, x2 * cos + x1 * sin], axis=-1).astype(x.dtype)


def diff_lambda(lq1, lk1, lq2, lk2, lam_init):
    a = jnp.sum(lq1.astype(jnp.float32) * lk1.astype(jnp.float32))
    b = jnp.sum(lq2.astype(jnp.float32) * lk2.astype(jnp.float32))
    return jnp.exp(a) - jnp.exp(b) + lam_init


def diff_mix(q, k, v, mask, lam):
    scale = DIFF_HD ** -0.5
    neg = jnp.finfo(jnp.float32).min
    s1 = jnp.einsum('bqhd,bkhd->bhqk', q[..., :DIFF_HD], k[..., :DIFF_HD]).astype(jnp.float32) * scale
    s2 = jnp.einsum('bqhd,bkhd->bhqk', q[..., DIFF_HD:], k[..., DIFF_HD:]).astype(jnp.float32) * scale
    p1 = jax.nn.softmax(jnp.where(mask, s1, neg), axis=-1)
    p2 = jax.nn.softmax(jnp.where(mask, s2, neg), axis=-1)
    w = (p1 - lam * p2).astype(v.dtype)
    return jnp.einsum('bhqk,bkhe->bqhe', w, v)


def diff_attention_prompt(q, k, v, lam):
    B, L, H, E = q.shape
    nb = L // Q_BLOCK
    qb = q.reshape(B, nb, Q_BLOCK, H, E).transpose(1, 0, 2, 3, 4)
    starts = jnp.arange(nb, dtype=jnp.int32) * Q_BLOCK
    kpos = jnp.arange(L, dtype=jnp.int32)

    def one_block(args):
        qi, s0 = args
        qpos = s0 + jnp.arange(Q_BLOCK, dtype=jnp.int32)
        return diff_mix(qi, k, v, kpos[None, :] <= qpos[:, None], lam)

    o = lax.map(one_block, (qb, starts))
    return o.transpose(1, 0, 2, 3, 4).reshape(B, L, H, E)


def diff_attention_sample(q, k_new, v_new, lam, cache_k, cache_v, page_table, layer):
    T = q.shape[1]
    past = page_table.shape[1] * PAGE_SIZE
    mask = jnp.concatenate([jnp.ones((T, past), bool), jnp.tril(jnp.ones((T, T), bool))], axis=1)

    def one_seq(args):
        qi, kn, vn, pt = args
        kp = cache_k[layer, pt].reshape(past, DIFF_HEADS, 2 * DIFF_HD).astype(kn.dtype)
        vp = cache_v[layer, pt].reshape(past, DIFF_HEADS, 2 * DIFF_HD).astype(vn.dtype)
        keys = jnp.concatenate([kp, kn], axis=0)[None]
        vals = jnp.concatenate([vp, vn], axis=0)[None]
        return diff_mix(qi[None], keys, vals, mask, lam)[0]

    return lax.map(one_seq, (q, k_new, v_new, page_table))


def retention_chunkwise(q, k, v, state0):
    B, L, H, DK = q.shape
    DV = v.shape[-1]
    C = RET_CHUNK if L % RET_CHUNK == 0 else L
    n = L // C
    lg = jnp.log(1.0 - 2.0 ** (-5.0 - jnp.arange(H, dtype=jnp.float32)))
    idx = jnp.arange(C, dtype=jnp.float32)
    diff = idx[:, None] - idx[None, :]
    dmask = jnp.where(diff >= 0, jnp.exp(jnp.maximum(diff, 0.0)[None] * lg[:, None, None]), 0.0)
    q_dec = jnp.exp((idx[:, None] + 1.0) * lg[None, :])
    k_dec = jnp.exp((C - 1.0 - idx[:, None]) * lg[None, :])
    c_dec = jnp.exp(C * lg)

    def to_chunks(t):
        return t.astype(jnp.float32).reshape(B, n, C, H, t.shape[-1]).transpose(1, 0, 2, 3, 4)

    def step(S, qkv):
        qc, kc, vc = qkv
        s = jnp.einsum('bihd,bjhd->bhij', qc, kc) * dmask
        o = jnp.einsum('bhij,bjhe->bihe', s, vc) + jnp.einsum('bihd,bhde->bihe', qc * q_dec[None, :, :, None], S)
        S = S * c_dec[None, :, None, None] + jnp.einsum('bjhd,bjhe->bhde', kc * k_dec[None, :, :, None], vc)
        return S, o

    S, o = lax.scan(step, state0.astype(jnp.float32), (to_chunks(q), to_chunks(k), to_chunks(v)))
    o = o.transpose(1, 0, 2, 3, 4).reshape(B, L, H, DV)
    return o.astype(v.dtype), S.astype(state0.dtype)


def clamped_swiglu(u):
    x_glu = jnp.minimum(u[..., ::2], SWIGLU_LIMIT)
    x_lin = jnp.clip(u[..., 1::2], -SWIGLU_LIMIT, SWIGLU_LIMIT)
    return x_glu * jax.nn.sigmoid(SWIGLU_ALPHA * x_glu) * (x_lin + 1.0)


def moe_ffn(h, w_r, b_r, w_up, b_up, w_dn, b_dn):
    N, D = h.shape
    E = N_EXPERTS
    logits = (h @ w_r).astype(jnp.float32) + b_r.astype(jnp.float32)
    top_v, top_i = lax.top_k(logits, TOP_K)
    gate = jax.nn.softmax(top_v, axis=-1)
    NK = N * TOP_K
    flat_e = top_i.reshape(-1).astype(jnp.int32)
    flat_tok = jnp.arange(NK, dtype=jnp.int32) // TOP_K
    flat_g = gate.reshape(-1)
    order = jnp.argsort(flat_e * NK + jnp.arange(NK, dtype=jnp.int32))
    se = flat_e[order]
    counts = jnp.zeros((E,), jnp.int32).at[flat_e].add(1)
    start = jnp.cumsum(counts) - counts
    pcounts = (counts + MOE_BLOCK - 1) // MOE_BLOCK * MOE_BLOCK
    pend = jnp.cumsum(pcounts)
    pstart = pend - pcounts
    dest = pstart[se] + jnp.arange(NK, dtype=jnp.int32) - start[se]
    n_blocks = (NK + E * (MOE_BLOCK - 1) + MOE_BLOCK - 1) // MOE_BLOCK
    P = n_blocks * MOE_BLOCK
    slot_tok = jnp.full((P,), N, jnp.int32).at[dest].set(flat_tok[order])
    slot_g = jnp.zeros((P,), jnp.float32).at[dest].set(flat_g[order])
    block_e = jnp.minimum(jnp.searchsorted(pend, jnp.arange(n_blocks, dtype=jnp.int32) * MOE_BLOCK, side='right'), E - 1)
    h_pad = jnp.concatenate([h, jnp.zeros((1, D), h.dtype)], axis=0)

    def run_block(args):
        tok, g, e = args
        xb = h_pad[tok]
        u = xb @ w_up[e] + b_up[e]
        y = clamped_swiglu(u) @ w_dn[e] + b_dn[e]
        return y * g[:, None].astype(y.dtype)

    yb = lax.map(run_block, (slot_tok.reshape(n_blocks, MOE_BLOCK), slot_g.reshape(n_blocks, MOE_BLOCK), block_e))
    out = jnp.zeros((N + 1, D), h.dtype).at[slot_tok].add(yb.reshape(P, D).astype(h.dtype))
    return out[:N]


def decoder_layer(x, c, pos, attend, ret_state0, lam, lam_init,
                  w_ada, b_ada, g_pre1, g_post1, g_pre2, g_post2, w_in, subln,
                  w_a, w_b, w_o, w_r, b_r, w_up, b_up, w_dn, b_dn):
    B, L, D = x.shape
    mod = jax.nn.silu(c) @ w_ada + b_ada
    sh1, sc1, gt1, sh2, sc2, gt2 = [m[:, None, :] for m in jnp.split(mod, 6, axis=-1)]
    h = rms_norm(x, g_pre1) * (1.0 + sc1) + sh1
    dq, dk, dv, rq, rk, rv, rg, ga, gb = split_columns(h @ w_in)
    dq = dq.reshape(B, L, DIFF_HEADS, 2 * DIFF_HD)
    dk = dk.reshape(B, L, DIFF_HEADS, 2 * DIFF_HD)
    dv = dv.reshape(B, L, DIFF_HEADS, 2 * DIFF_HD)
    o_a = attend(dq, dk, dv, lam)
    o_a = (rms_norm(o_a, subln, SUBLN_EPS) * (1.0 - lam_init)).reshape(B, L, DIFF_V)
    rq = rotary(rq.reshape(B, L, RET_HEADS, RET_DK), pos)
    rk = rotary(rk.reshape(B, L, RET_HEADS, RET_DK), pos) * (RET_DK ** -0.5)
    o_b, ret_state = retention_chunkwise(rq, rk, rv.reshape(B, L, RET_HEADS, RET_DV), ret_state0)
    o_b = rms_norm(o_b).reshape(B, L, RET_V) * jax.nn.silu(rg)
    merged = jax.nn.sigmoid(ga) * (o_a @ w_a) + jax.nn.sigmoid(gb) * (o_b @ w_b)
    x = x + gt1 * rms_norm(merged @ w_o, g_post1)
    h2 = rms_norm(x, g_pre2) * (1.0 + sc2) + sh2
    f = moe_ffn(h2.reshape(B * L, D), w_r, b_r, w_up, b_up, w_dn, b_dn).reshape(B, L, D)
    x = x + gt2 * rms_norm(f, g_post2)
    return x, dk, dv, ret_state


def setup_inputs(seed: int = 0) -> dict:
    key = jax.random.key(seed)
    ks = jax.random.split(key, 32)
    n_pages = PAST_LEN // PAGE_SIZE
    n_phys = (5 * DEC_BATCH * n_pages + 3) // 4
    f32 = jnp.float32
    nrm = lambda k, shape, s: jax.random.normal(k, shape, f32) * s
    page_table = jax.random.permutation(ks[5], n_phys)[:DEC_BATCH * n_pages].reshape(DEC_BATCH, n_pages).astype(jnp.int32)
    return {
        'x_prompt': nrm(ks[0], (BATCH, SEQ, D_MODEL), 1.0),
        'x_sample': nrm(ks[1], (DEC_BATCH, DEC_SEQ, D_MODEL), 1.0),
        'cache_k': nrm(ks[2], (DEPTH, n_phys, PAGE_SIZE, DIFF_HEADS, 2 * DIFF_HD), 1.0),
        'cache_v': nrm(ks[3], (DEPTH, n_phys, PAGE_SIZE, DIFF_HEADS, 2 * DIFF_HD), 1.0),
        'state_ret': nrm(ks[4], (DEPTH, DEC_BATCH, RET_HEADS, RET_DK, RET_DV), 0.5),
        'page_table': page_table,
        'c_prompt': nrm(ks[6], (BATCH, D_MODEL), 1.0),
        'c_sample': nrm(ks[7], (DEC_BATCH, D_MODEL), 1.0),
        'w_ada': nrm(ks[8], (DEPTH, D_MODEL, 6 * D_MODEL), D_MODEL ** -0.5),
        'b_ada': nrm(ks[9], (DEPTH, 6 * D_MODEL), 0.01),
        'norm_pre_mix': 1.0 + nrm(ks[10], (DEPTH, D_MODEL), 0.1),
        'norm_post_mix': 1.0 + nrm(ks[11], (DEPTH, D_MODEL), 0.1),
        'norm_pre_ffn': 1.0 + nrm(ks[12], (DEPTH, D_MODEL), 0.1),
        'norm_post_ffn': 1.0 + nrm(ks[13], (DEPTH, D_MODEL), 0.1),
        'w_in': nrm(ks[14], (DEPTH, D_MODEL, IN_WIDTH), D_MODEL ** -0.5),
        'lambda_q1': nrm(ks[15], (DEPTH, DIFF_HD), 0.1),
        'lambda_k1': nrm(ks[16], (DEPTH, DIFF_HD), 0.1),
        'lambda_q2': nrm(ks[17], (DEPTH, DIFF_HD), 0.1),
        'lambda_k2': nrm(ks[18], (DEPTH, DIFF_HD), 0.1),
        'diff_subln': 1.0 + nrm(ks[19], (DEPTH, 2 * DIFF_HD), 0.1),
        'w_branch_a': nrm(ks[20], (DEPTH, DIFF_V, D_MODEL), DIFF_V ** -0.5),
        'w_branch_b': nrm(ks[21], (DEPTH, RET_V, D_MODEL), RET_V ** -0.5),
        'w_out': nrm(ks[22], (DEPTH, D_MODEL, D_MODEL), D_MODEL ** -0.5),
        'w_router': nrm(ks[23], (DEPTH, D_MODEL, N_EXPERTS), D_MODEL ** -0.5),
        'b_router': nrm(ks[24], (DEPTH, N_EXPERTS), 0.01),
        'w_exp_up': nrm(ks[25], (DEPTH, N_EXPERTS, D_MODEL, 2 * D_FF), D_MODEL ** -0.5),
        'b_exp_up': nrm(ks[26], (DEPTH, N_EXPERTS, 2 * D_FF), 0.01),
        'w_exp_down': nrm(ks[27], (DEPTH, N_EXPERTS, D_FF, D_MODEL), D_FF ** -0.5),
        'b_exp_down': nrm(ks[28], (DEPTH, N_EXPERTS, D_MODEL), 0.01),
    }


def reference(x_prompt, x_sample, cache_k, cache_v, state_ret, page_table, c_prompt, c_sample,
              w_ada, b_ada, norm_pre_mix, norm_post_mix, norm_pre_ffn, norm_post_ffn, w_in,
              lambda_q1, lambda_k1, lambda_q2, lambda_k2, diff_subln, w_branch_a, w_branch_b, w_out,
              w_router, b_router, w_exp_up, b_exp_up, w_exp_down, b_exp_down):
    past = page_table.shape[1] * PAGE_SIZE
    pos_p = jnp.arange(x_prompt.shape[1], dtype=jnp.int32)
    pos_s = past + jnp.arange(x_sample.shape[1], dtype=jnp.int32)
    yp, ys = x_prompt, x_sample
    kp_l, vp_l, ks_l, vs_l, rp_l, rs_l = [], [], [], [], [], []
    for l in range(DEPTH):
        lam_init = 0.8 - 0.6 * math.exp(-0.3 * l)
        lam = diff_lambda(lambda_q1[l], lambda_k1[l], lambda_q2[l], lambda_k2[l], lam_init)
        weights = (w_ada[l], b_ada[l], norm_pre_mix[l], norm_post_mix[l], norm_pre_ffn[l], norm_post_ffn[l],
                   w_in[l], diff_subln[l], w_branch_a[l], w_branch_b[l], w_out[l],
                   w_router[l], b_router[l], w_exp_up[l], b_exp_up[l], w_exp_down[l], b_exp_down[l])
        ret0 = jnp.zeros((x_prompt.shape[0], RET_HEADS, RET_DK, RET_DV), x_prompt.dtype)
        yp, kp, vp, rp = decoder_layer(yp, c_prompt, pos_p, diff_attention_prompt, ret0, lam, lam_init, *weights)
        attend_s = functools.partial(diff_attention_sample, cache_k=cache_k, cache_v=cache_v,
                                     page_table=page_table, layer=l)
        ys, ks, vs, rs = decoder_layer(ys, c_sample, pos_s, attend_s, state_ret[l], lam, lam_init, *weights)
        kp_l.append(kp)
        vp_l.append(vp)
        ks_l.append(ks)
        vs_l.append(vs)
        rp_l.append(rp)
        rs_l.append(rs)
    return (yp, ys, jnp.stack(kp_l), jnp.stack(vp_l), jnp.stack(ks_l), jnp.stack(vs_l), jnp.stack(rp_l), jnp.stack(rs_l))
```

```python
import functools
import math

import jax
import jax.numpy as jnp
from jax import lax
from jax.experimental import pallas as pl
from jax.experimental.pallas import tpu as pltpu

F32 = jnp.float32
BF16 = jnp.bfloat16

DIFF_HEADS = 8
DIFF_HD = 64
HEAD_W = 2 * DIFF_HD
RET_HEADS = 4
PAGE_SIZE = 128
TOP_K = 4
ROPE_BASE = 10000.0
SWIGLU_LIMIT = 7.0
SWIGLU_ALPHA = 1.702
EPS = 1e-6
SUBLN_EPS = 1e-5
LANES = 128
VMEM_LIMIT = 52 * 1024 * 1024
NEG = -1e30

_NT = (((1,), (1,)), ((), ()))


def _cparams(sem):
    return pltpu.CompilerParams(dimension_semantics=sem, vmem_limit_bytes=VMEM_LIMIT)


def _sigmoid(x):
    return 1.0 / (1.0 + jnp.exp(-x))


def _rms(x, eps):
    return x * lax.rsqrt(jnp.mean(x * x, axis=-1, keepdims=True) + eps)


def _dot(a, b):
    return jnp.dot(a, b, preferred_element_type=F32)


def _ada_kernel(c_ref, w_ref, b_ref, o_ref):
    c = c_ref[...]
    s = (c * _sigmoid(c)).astype(BF16)
    o_ref[...] = _dot(s, w_ref[...].astype(BF16)) + b_ref[...]


def _ada(c, w, b, tn=1024):
    r, d = c.shape
    n = w.shape[1]
    return pl.pallas_call(
        _ada_kernel,
        grid=(n // tn,),
        in_specs=[pl.BlockSpec((r, d), lambda j: (0, 0)),
                  pl.BlockSpec((d, tn), lambda j: (0, j)),
                  pl.BlockSpec((1, tn), lambda j: (0, j))],
        out_specs=pl.BlockSpec((r, tn), lambda j: (0, j)),
        out_shape=jax.ShapeDtypeStruct((r, n), F32),
        compiler_params=_cparams(("arbitrary",)),
        name="ada_mod",
    )(c, w, b.reshape(1, n))


class _Mod:
    def __init__(self, table, d, tm, rows_per_entry):
        self.d = d
        self.tm = tm
        if rows_per_entry == 1:
            self.arr = table
            self.per_row = True
        else:
            self.arr = table.reshape(table.shape[0], 1, table.shape[1])
            self.per_row = False
            self.tiles_per_entry = rows_per_entry // tm

    def spec(self, c):
        if self.per_row:
            return pl.BlockSpec((self.tm, self.d), lambda i: (i, c))
        tpe = self.tiles_per_entry
        return pl.BlockSpec((None, 1, self.d), lambda i: (i // tpe, 0, c))


def _prenorm_kernel(x_ref, g_ref, sc_ref, sh_ref, o_ref):
    y = _rms(x_ref[...], EPS) * g_ref[...]
    o_ref[...] = (y * (1.0 + sc_ref[...]) + sh_ref[...]).astype(o_ref.dtype)


def _prenorm(x, g, mod, c_scale, c_shift):
    n, d = x.shape
    tm = mod.tm
    return pl.pallas_call(
        _prenorm_kernel,
        grid=(n // tm,),
        in_specs=[pl.BlockSpec((tm, d), lambda i: (i, 0)),
                  pl.BlockSpec((1, d), lambda i: (0, 0)),
                  mod.spec(c_scale), mod.spec(c_shift)],
        out_specs=pl.BlockSpec((tm, d), lambda i: (i, 0)),
        out_shape=jax.ShapeDtypeStruct((n, d), BF16),
        compiler_params=_cparams(("arbitrary",)),
        name="prenorm",
    )(x, g.reshape(1, d), mod.arr, mod.arr)


def _mm_kernel(h_ref, w_ref, *rest):
    outs, wb = rest[:-1], rest[-1]

    @pl.when(pl.program_id(1) == 0)
    def _():
        wb[...] = w_ref[...].astype(BF16)

    acc = _dot(h_ref[...], wb[...])
    for o in outs:
        o[...] = acc.astype(o.dtype)


def _matmul(h, w, col0, ncols, out_dtypes, tm, tn=1024):
    n, k = h.shape
    off = col0 // tn
    outs = pl.pallas_call(
        _mm_kernel,
        grid=(ncols // tn, n // tm),
        in_specs=[pl.BlockSpec((tm, k), lambda j, i: (i, 0)),
                  pl.BlockSpec((k, tn), lambda j, i: (0, j + off))],
        out_specs=[pl.BlockSpec((tm, tn), lambda j, i: (i, j)) for _ in out_dtypes],
        out_shape=[jax.ShapeDtypeStruct((n, ncols), dt) for dt in out_dtypes],
        scratch_shapes=[pltpu.VMEM((k, tn), BF16)],
        compiler_params=_cparams(("arbitrary", "arbitrary")),
        name="in_proj",
    )(h, w)
    return outs


def _lambda(lq1, lk1, lq2, lk2, lam_init):
    a = jnp.sum(lq1[...] * lk1[...], axis=-1, keepdims=True)
    b = jnp.sum(lq2[...] * lk2[...], axis=-1, keepdims=True)
    return jnp.exp(a) - jnp.exp(b) + lam_init


def _stack_components(q):
    lane = lax.broadcasted_iota(jnp.int32, q.shape, 1)
    zero = jnp.zeros_like(q)
    return jnp.concatenate([jnp.where(lane < DIFF_HD, q, zero),
                            jnp.where(lane >= DIFF_HD, q, zero)], axis=0)


def _attn_prompt_kernel(lq1, lk1, lq2, lk2, q_ref, k_ref, v_ref, sub_ref, o_ref,
                        qs, m_s, l_s, acc, *, t, lam_init):
    qi = pl.program_id(2)
    qs[...] = _stack_components(q_ref[...].astype(F32) * (DIFF_HD ** -0.5)).astype(BF16)
    m_s[...] = jnp.full(m_s.shape, -jnp.inf, F32)
    l_s[...] = jnp.zeros(l_s.shape, F32)
    acc[...] = jnp.zeros(acc.shape, F32)

    def step(ki, masked):
        start = pl.multiple_of(ki * t, t)
        k = k_ref[pl.ds(start, t), :]
        v = v_ref[pl.ds(start, t), :]
        s = lax.dot_general(qs[...], k, _NT, preferred_element_type=F32)
        if masked:
            row = lax.broadcasted_iota(jnp.int32, s.shape, 0)
            col = lax.broadcasted_iota(jnp.int32, s.shape, 1)
            row = jnp.where(row >= t, row - t, row)
            s = jnp.where(col <= row, s, -jnp.inf)
        m_new = jnp.maximum(m_s[...], jnp.max(s, axis=-1, keepdims=True))
        alpha = jnp.exp(m_s[...] - m_new)
        p = jnp.exp(s - m_new)
        l_s[...] = alpha * l_s[...] + jnp.sum(p, axis=-1, keepdims=True)
        acc[...] = alpha * acc[...] + _dot(p.astype(BF16), v)
        m_s[...] = m_new

    def body(ki, carry):
        step(ki, False)
        return carry

    lax.fori_loop(0, qi, body, 0)
    step(qi, True)

    lam = _lambda(lq1, lk1, lq2, lk2, lam_init)
    o = acc[...] / l_s[...]
    o = o[:t] - lam * o[t:]
    o = _rms(o, SUBLN_EPS) * sub_ref[...] * (1.0 - lam_init)
    o_ref[...] = o.astype(o_ref.dtype)


def _attn_prompt(q, k, v, lams, subln, batch, seq, lam_init, t):
    n, w = q.shape
    nq = seq // t
    lam_specs = [pl.BlockSpec((1, DIFF_HD), lambda b, h, i: (0, 0)) for _ in range(4)]
    return pl.pallas_call(
        functools.partial(_attn_prompt_kernel, t=t, lam_init=lam_init),
        grid=(batch, DIFF_HEADS, nq),
        in_specs=lam_specs + [
            pl.BlockSpec((t, HEAD_W), lambda b, h, i: (b * nq + i, h)),
            pl.BlockSpec((seq, HEAD_W), lambda b, h, i: (b, h)),
            pl.BlockSpec((seq, HEAD_W), lambda b, h, i: (b, h)),
            pl.BlockSpec((1, HEAD_W), lambda b, h, i: (0, 0))],
        out_specs=pl.BlockSpec((t, HEAD_W), lambda b, h, i: (b * nq + i, h)),
        out_shape=jax.ShapeDtypeStruct((n, w), BF16),
        scratch_shapes=[pltpu.VMEM((2 * t, HEAD_W), BF16),
                        pltpu.VMEM((2 * t, 1), F32),
                        pltpu.VMEM((2 * t, 1), F32),
                        pltpu.VMEM((2 * t, HEAD_W), F32)],
        compiler_params=_cparams(("arbitrary", "arbitrary", "arbitrary")),
        name="diff_attn_prompt",
    )(*lams, q, k, v, subln.reshape(1, HEAD_W))


def _attn_decode_kernel(pt_ref, lq1, lk1, lq2, lk2, q_ref, kn_ref, vn_ref, sub_ref, *rest,
                        pages_per_step, lam_init):
    del pt_ref
    k_refs = rest[:pages_per_step]
    v_refs = rest[pages_per_step:2 * pages_per_step]
    o_ref, qbd, m_s, l_s, acc = rest[2 * pages_per_step:]
    p_idx = pl.program_id(1)
    rows = 2 * DIFF_HEADS

    @pl.when(p_idx == 0)
    def _():
        q = q_ref[...].astype(F32) * (DIFF_HD ** -0.5)
        qb = jnp.broadcast_to(q, (rows, q.shape[1]))
        row = lax.broadcasted_iota(jnp.int32, qb.shape, 0)
        col = lax.broadcasted_iota(jnp.int32, qb.shape, 1)
        qbd[...] = jnp.where(col // DIFF_HD == row, qb, 0.0).astype(BF16)
        m_s[...] = jnp.full(m_s.shape, -jnp.inf, F32)
        l_s[...] = jnp.zeros(l_s.shape, F32)
        acc[...] = jnp.zeros(acc.shape, F32)

    def online(s, pv_fn):
        m_new = jnp.maximum(m_s[...], jnp.max(s, axis=-1, keepdims=True))
        alpha = jnp.exp(m_s[...] - m_new)
        p = jnp.exp(s - m_new)
        l_s[...] = alpha * l_s[...] + jnp.sum(p, axis=-1, keepdims=True)
        acc[...] = alpha * acc[...] + pv_fn(p)
        m_s[...] = m_new

    for k_ref, v_ref in zip(k_refs, v_refs):
        kb = k_ref[...].astype(BF16)
        s = lax.dot_general(qbd[...], kb, _NT, preferred_element_type=F32)
        online(s, lambda p, v_ref=v_ref: _dot(p.astype(BF16), v_ref[...].astype(BF16)))

    @pl.when(p_idx == pl.num_programs(1) - 1)
    def _():
        kn = kn_ref[...].astype(F32)
        vn = vn_ref[...].astype(F32)
        s = jnp.sum(qbd[...].astype(F32) * kn, axis=-1, keepdims=True)
        online(s, lambda p: p * vn)
        lam = _lambda(lq1, lk1, lq2, lk2, lam_init)
        o = acc[...] / l_s[...]
        for h in range(DIFF_HEADS):
            cols = slice(h * HEAD_W, (h + 1) * HEAD_W)
            oh = o[2 * h:2 * h + 1, cols] - lam * o[2 * h + 1:2 * h + 2, cols]
            oh = _rms(oh, SUBLN_EPS) * sub_ref[...] * (1.0 - lam_init)
            o_ref[:, cols] = oh.astype(o_ref.dtype)


def _attn_decode(q, k_new, v_new, cache_k, cache_v, page_base, page_table, lams, subln,
                 lam_init, pages_per_step):
    nb, d = q.shape
    n_pages = page_table.shape[1]
    g = pages_per_step
    steps = n_pages // g
    pt_flat = (page_table + page_base).reshape(-1).astype(jnp.int32)

    def page_spec(r):
        return pl.BlockSpec((None, PAGE_SIZE, d),
                            lambda b, p, pt: (pt[b * n_pages + p * g + r], 0, 0))

    row_spec = pl.BlockSpec((None, 1, d), lambda b, p, pt: (b, 0, 0))
    lam_specs = [pl.BlockSpec((1, DIFF_HD), lambda b, p, pt: (0, 0)) for _ in range(4)]
    grid_spec = pltpu.PrefetchScalarGridSpec(
        num_scalar_prefetch=1,
        grid=(nb, steps),
        in_specs=lam_specs + [row_spec, row_spec, row_spec,
                              pl.BlockSpec((1, HEAD_W), lambda b, p, pt: (0, 0))]
        + [page_spec(r) for r in range(g)] + [page_spec(r) for r in range(g)],
        out_specs=row_spec,
        scratch_shapes=[pltpu.VMEM((2 * DIFF_HEADS, d), BF16),
                        pltpu.VMEM((2 * DIFF_HEADS, 1), F32),
                        pltpu.VMEM((2 * DIFF_HEADS, 1), F32),
                        pltpu.VMEM((2 * DIFF_HEADS, d), F32)])
    out = pl.pallas_call(
        functools.partial(_attn_decode_kernel, pages_per_step=g, lam_init=lam_init),
        grid_spec=grid_spec,
        out_shape=jax.ShapeDtypeStruct((nb, 1, d), BF16),
        compiler_params=_cparams(("arbitrary", "arbitrary")),
        name="diff_attn_decode",
    )(pt_flat, *lams, q.reshape(nb, 1, d), k_new.reshape(nb, 1, d), v_new.reshape(nb, 1, d),
      subln.reshape(1, HEAD_W), *([cache_k] * g), *([cache_v] * g))
    return out.reshape(nb, d)


def _rotate(x, cos, sin):
    half = x.shape[-1] // 2
    x1, x2 = x[:, :half], x[:, half:]
    return jnp.concatenate([x1 * cos - x2 * sin, x2 * cos + x1 * sin], axis=-1)


def _ret_prompt_kernel(lg_ref, q_ref, k_ref, v_ref, g_ref, cos_ref, sin_ref, o_ref, st_ref,
                       state, dmask, *, c):
    h = pl.program_id(1)
    ci = pl.program_id(2)
    lg = lg_ref[h]
    idx = lax.broadcasted_iota(jnp.int32, (c, 1), 0).astype(F32)

    @pl.when(ci == 0)
    def _():
        state[...] = jnp.zeros(state.shape, F32)
        row = lax.broadcasted_iota(jnp.int32, (c, c), 0)
        col = lax.broadcasted_iota(jnp.int32, (c, c), 1)
        diff = (row - col).astype(F32)
        dmask[...] = jnp.where(diff >= 0, jnp.exp(jnp.maximum(diff, 0.0) * lg), 0.0)

    cos, sin = cos_ref[...], sin_ref[...]
    q = _rotate(q_ref[...].astype(F32), cos, sin)
    k = _rotate(k_ref[...].astype(F32), cos, sin) * (q.shape[-1] ** -0.5)
    v = v_ref[...]
    q_dec = jnp.exp((idx + 1.0) * lg)
    k_dec = jnp.exp((c - 1.0 - idx) * lg)
    c_dec = jnp.exp(jnp.full((1, 1), c * lg, F32))

    s = lax.dot_general(q.astype(BF16), k.astype(BF16), _NT, preferred_element_type=F32) * dmask[...]
    st = state[...]
    o = _dot(s.astype(BF16), v) + _dot((q * q_dec).astype(BF16), st.astype(BF16))
    kd_t = (k * k_dec).T.astype(BF16)
    st = st * c_dec + _dot(kd_t, v)
    state[...] = st

    o = _rms(o, EPS)
    gate = g_ref[...].astype(F32)
    o_ref[...] = (o * (gate * _sigmoid(gate))).astype(o_ref.dtype)

    @pl.when(ci == pl.num_programs(2) - 1)
    def _():
        st_ref[...] = st


def _ret_prompt(z, col_q, col_k, col_v, col_g, batch, seq, cos, sin, lg, c):
    n = z.shape[0]
    dk = (col_k - col_q) // RET_HEADS
    dv = (col_g - col_v) // RET_HEADS
    nc = seq // c
    qo, ko, vo, go = col_q // dk, col_k // dk, col_v // dv, col_g // dv
    grid_spec = pltpu.PrefetchScalarGridSpec(
        num_scalar_prefetch=0,
        grid=(batch, RET_HEADS, nc),
        in_specs=[pl.BlockSpec(memory_space=pltpu.SMEM),
                  pl.BlockSpec((c, dk), lambda b, h, i: (b * nc + i, qo + h)),
                  pl.BlockSpec((c, dk), lambda b, h, i: (b * nc + i, ko + h)),
                  pl.BlockSpec((c, dv), lambda b, h, i: (b * nc + i, vo + h)),
                  pl.BlockSpec((c, dv), lambda b, h, i: (b * nc + i, go + h)),
                  pl.BlockSpec((c, dk // 2), lambda b, h, i: (i, 0)),
                  pl.BlockSpec((c, dk // 2), lambda b, h, i: (i, 0))],
        out_specs=[pl.BlockSpec((c, dv), lambda b, h, i: (b * nc + i, h)),
                   pl.BlockSpec((None, None, dk, dv), lambda b, h, i: (b, h, 0, 0))],
        scratch_shapes=[pltpu.VMEM((dk, dv), F32), pltpu.VMEM((c, c), F32)])
    return pl.pallas_call(
        functools.partial(_ret_prompt_kernel, c=c),
        grid_spec=grid_spec,
        out_shape=[jax.ShapeDtypeStruct((n, RET_HEADS * dv), BF16),
                   jax.ShapeDtypeStruct((batch, RET_HEADS, dk, dv), F32)],
        compiler_params=_cparams(("arbitrary", "arbitrary", "arbitrary")),
        name="retention_prompt",
    )(lg, z, z, z, z, cos, sin)


def _ret_decode_kernel(lg_ref, q_ref, k_ref, v_ref, g_ref, cos_ref, sin_ref, st_ref, o_ref, so_ref):
    dk, dv = st_ref.shape[1], st_ref.shape[2]
    cos, sin = cos_ref[...], sin_ref[...]
    for h in range(RET_HEADS):
        gamma = jnp.exp(jnp.full((1, 1), lg_ref[h], F32))
        q = _rotate(q_ref[:, h * dk:(h + 1) * dk].astype(F32), cos, sin)
        k = _rotate(k_ref[:, h * dk:(h + 1) * dk].astype(F32), cos, sin) * (dk ** -0.5)
        v = v_ref[:, h * dv:(h + 1) * dv].astype(F32)
        st = st_ref[h]
        q_col = jnp.broadcast_to(q * gamma, (LANES, dk)).T[:, 0:1]
        k_col = jnp.broadcast_to(k, (LANES, dk)).T[:, 0:1]
        qk = jnp.sum(q * k, axis=-1, keepdims=True)
        o = qk * v + jnp.sum(q_col * st, axis=0, keepdims=True)
        so_ref[h] = st * gamma + k_col * v
        o = _rms(o, EPS)
        gate = g_ref[:, h * dv:(h + 1) * dv].astype(F32)
        o_ref[:, h * dv:(h + 1) * dv] = (o * (gate * _sigmoid(gate))).astype(o_ref.dtype)


def _ret_decode(z, col_q, col_k, col_v, col_g, state, cos, sin, lg):
    nb = z.shape[0]
    _, _, dk, dv = state.shape
    z3 = z.reshape(nb, 1, z.shape[1])
    wq, wv = RET_HEADS * dk, RET_HEADS * dv

    def row(width, col):
        return pl.BlockSpec((None, 1, width), lambda b: (b, 0, col // width))

    o, st = pl.pallas_call(
        _ret_decode_kernel,
        grid=(nb,),
        in_specs=[pl.BlockSpec(memory_space=pltpu.SMEM),
                  row(wq, col_q), row(wq, col_k), row(wv, col_v), row(wv, col_g),
                  pl.BlockSpec((1, dk // 2), lambda b: (0, 0)),
                  pl.BlockSpec((1, dk // 2), lambda b: (0, 0)),
                  pl.BlockSpec((None, RET_HEADS, dk, dv), lambda b: (b, 0, 0, 0))],
        out_specs=[pl.BlockSpec((None, 1, wv), lambda b: (b, 0, 0)),
                   pl.BlockSpec((None, RET_HEADS, dk, dv), lambda b: (b, 0, 0, 0))],
        out_shape=[jax.ShapeDtypeStruct((nb, 1, wv), BF16),
                   jax.ShapeDtypeStruct(state.shape, F32)],
        compiler_params=_cparams(("arbitrary",)),
        name="retention_decode",
    )(lg, z3, z3, z3, z3, cos, sin, state)
    return o.reshape(nb, wv), st


def _postmix_kernel(oa_ref, ob_ref, ga_ref, gb_ref, x_ref, gt1_ref, sc2_ref, sh2_ref,
                    wa_ref, wb_ref, wo_ref, gpost_ref, gpre_ref, wr_ref, br_ref,
                    x1_ref, h2_ref, ti_ref, tg_ref):
    a = _dot(oa_ref[...], wa_ref[...])
    b = _dot(ob_ref[...], wb_ref[...])
    merged = _sigmoid(ga_ref[...].astype(F32)) * a + _sigmoid(gb_ref[...].astype(F32)) * b
    y = _dot(merged.astype(BF16), wo_ref[...])
    x1 = x_ref[...] + gt1_ref[...] * (_rms(y, EPS) * gpost_ref[...])
    x1_ref[...] = x1
    h2 = _rms(x1, EPS) * gpre_ref[...] * (1.0 + sc2_ref[...]) + sh2_ref[...]
    h2_ref[...] = h2

    logits = jnp.dot(h2, wr_ref[...], preferred_element_type=F32,
                     precision=lax.Precision.HIGHEST) + br_ref[...]
    lane = lax.broadcasted_iota(jnp.int32, logits.shape, 1)
    cur = logits
    vals, idxs = [], []
    for _ in range(TOP_K):
        m = jnp.max(cur, axis=-1, keepdims=True)
        idx = jnp.min(jnp.where(cur == m, lane, LANES), axis=-1, keepdims=True)
        vals.append(m)
        idxs.append(idx)
        cur = jnp.where(lane == idx, NEG * 2.0, cur)
    exps = [jnp.exp(v - vals[0]) for v in vals]
    denom = exps[0]
    for e in exps[1:]:
        denom = denom + e
    ti = jnp.zeros(logits.shape, jnp.int32)
    tg = jnp.zeros(logits.shape, F32)
    for kk in range(TOP_K):
        ti = jnp.where(lane == kk, idxs[kk], ti)
        tg = jnp.where(lane == kk, exps[kk] / denom, tg)
    ti_ref[...] = ti
    tg_ref[...] = tg


def _postmix(o_a, o_b, z, col_ga, col_gb, x, mod, wa, wb, wo, gpost, gpre, wr, br):
    n, d = x.shape
    tm = mod.tm
    da, db = o_a.shape[1], o_b.shape[1]
    const = lambda shape: pl.BlockSpec(shape, lambda i: (0, 0))
    rows = lambda w: pl.BlockSpec((tm, w), lambda i: (i, 0))
    return pl.pallas_call(
        _postmix_kernel,
        grid=(n // tm,),
        in_specs=[rows(da), rows(db),
                  pl.BlockSpec((tm, d), lambda i: (i, col_ga // d)),
                  pl.BlockSpec((tm, d), lambda i: (i, col_gb // d)),
                  rows(d), mod.spec(2), mod.spec(4), mod.spec(3),
                  const((da, d)), const((db, d)), const((d, d)),
                  const((1, d)), const((1, d)), const((d, LANES)), const((1, LANES))],
        out_specs=[rows(d), rows(d), rows(LANES), rows(LANES)],
        out_shape=[jax.ShapeDtypeStruct((n, d), F32), jax.ShapeDtypeStruct((n, d), F32),
                   jax.ShapeDtypeStruct((n, LANES), jnp.int32),
                   jax.ShapeDtypeStruct((n, LANES), F32)],
        compiler_params=_cparams(("arbitrary",)),
        name="postmix_router",
    )(o_a, o_b, z, z, x, mod.arr, mod.arr, mod.arr, wa, wb, wo,
      gpost.reshape(1, d), gpre.reshape(1, d), wr, br)


def _row_copy(src_hbm, row, dst, r, sem):
    return pltpu.make_async_copy(src_hbm.at[pl.ds(row, 1)], dst.at[pl.ds(r, 1)], sem)


def _gather_kernel(tok_ref, h_hbm, o_ref, sem, *, tm):
    base = pl.program_id(0) * tm

    def issue(r, carry):
        _row_copy(h_hbm, tok_ref[base + r], o_ref, r, sem).start()
        return carry

    lax.fori_loop(0, tm, issue, 0, unroll=8)

    def wait(r, carry):
        _row_copy(h_hbm, 0, o_ref, r, sem).wait()
        return carry

    lax.fori_loop(0, tm, wait, 0, unroll=8)


def _gather_rows(h, slot_tok, tm):
    p = slot_tok.shape[0]
    d = h.shape[1]
    grid_spec = pltpu.PrefetchScalarGridSpec(
        num_scalar_prefetch=1,
        grid=(p // tm,),
        in_specs=[pl.BlockSpec(memory_space=pl.ANY)],
        out_specs=pl.BlockSpec((tm, d), lambda t, tok: (t, 0)),
        scratch_shapes=[pltpu.SemaphoreType.DMA(())])
    return pl.pallas_call(
        functools.partial(_gather_kernel, tm=tm),
        grid_spec=grid_spec,
        out_shape=jax.ShapeDtypeStruct((p, d), h.dtype),
        compiler_params=_cparams(("arbitrary",)),
        name="moe_gather",
    )(slot_tok, h)


def _ffn_kernel(te_ref, nv_ref, x_ref, wg_ref, wl_ref, bg_ref, bl_ref, wd_ref, bd_ref, o_ref):
    del te_ref
    t = pl.program_id(0)

    @pl.when(t < nv_ref[0])
    def _():
        x = x_ref[...].astype(BF16)
        g = jnp.minimum(_dot(x, wg_ref[...]) + bg_ref[...], SWIGLU_LIMIT)
        l = jnp.clip(_dot(x, wl_ref[...]) + bl_ref[...], -SWIGLU_LIMIT, SWIGLU_LIMIT)
        act = g * _sigmoid(SWIGLU_ALPHA * g) * (l + 1.0)
        o_ref[...] = _dot(act.astype(BF16), wd_ref[...]) + bd_ref[...]

    @pl.when(t >= nv_ref[0])
    def _():
        o_ref[...] = jnp.zeros(o_ref.shape, o_ref.dtype)


def _expert_ffn(xg, tile_expert, n_valid, w_up, b_up, w_dn, b_dn, tm):
    p, d = xg.shape
    dff = w_up.shape[-1]
    e = w_up.shape[0]
    grid_spec = pltpu.PrefetchScalarGridSpec(
        num_scalar_prefetch=2,
        grid=(p // tm,),
        in_specs=[pl.BlockSpec((tm, d), lambda t, te, nv: (t, 0)),
                  pl.BlockSpec((None, None, d, dff), lambda t, te, nv: (te[t], 0, 0, 0)),
                  pl.BlockSpec((None, None, d, dff), lambda t, te, nv: (te[t], 1, 0, 0)),
                  pl.BlockSpec((None, None, 1, dff), lambda t, te, nv: (te[t], 0, 0, 0)),
                  pl.BlockSpec((None, None, 1, dff), lambda t, te, nv: (te[t], 1, 0, 0)),
                  pl.BlockSpec((None, dff, d), lambda t, te, nv: (te[t], 0, 0)),
                  pl.BlockSpec((None, 1, d), lambda t, te, nv: (te[t], 0, 0))],
        out_specs=pl.BlockSpec((tm, d), lambda t, te, nv: (t, 0)))
    return pl.pallas_call(
        _ffn_kernel,
        grid_spec=grid_spec,
        out_shape=jax.ShapeDtypeStruct((p, d), F32),
        compiler_params=_cparams(("arbitrary",)),
        name="moe_expert_ffn",
    )(tile_expert, n_valid, xg, w_up, w_up, b_up.reshape(e, 2, 1, dff), b_up.reshape(e, 2, 1, dff),
      w_dn, b_dn.reshape(e, 1, d))


def _combine_kernel(pos_ref, y_hbm, tg_ref, x1_ref, gt2_ref, gpost_ref, o_ref, buf, sem, *, tm):
    base = pl.program_id(0) * tm * TOP_K

    def issue(r, carry):
        for kk in range(TOP_K):
            _row_copy(y_hbm, pos_ref[base + r * TOP_K + kk], buf.at[kk], r, sem).start()
        return carry

    lax.fori_loop(0, tm, issue, 0, unroll=4)

    def wait(r, carry):
        for kk in range(TOP_K):
            _row_copy(y_hbm, 0, buf.at[kk], r, sem).wait()
        return carry

    lax.fori_loop(0, tm, wait, 0, unroll=4)

    tg = tg_ref[...]
    f = tg[:, 0:1] * buf[0]
    for kk in range(1, TOP_K):
        f = f + tg[:, kk:kk + 1] * buf[kk]
    o_ref[...] = x1_ref[...] + gt2_ref[...] * (_rms(f, EPS) * gpost_ref[...])


def _combine(ys, pos, tg, x1, mod, gpost):
    n, d = x1.shape
    tm = mod.tm
    grid_spec = pltpu.PrefetchScalarGridSpec(
        num_scalar_prefetch=1,
        grid=(n // tm,),
        in_specs=[pl.BlockSpec(memory_space=pl.ANY),
                  pl.BlockSpec((tm, LANES), lambda i, pos: (i, 0)),
                  pl.BlockSpec((tm, d), lambda i, pos: (i, 0)),
                  _with_prefetch(mod.spec(5)),
                  pl.BlockSpec((1, d), lambda i, pos: (0, 0))],
        out_specs=pl.BlockSpec((tm, d), lambda i, pos: (i, 0)),
        scratch_shapes=[pltpu.VMEM((TOP_K, tm, d), F32), pltpu.SemaphoreType.DMA(())])
    return pl.pallas_call(
        functools.partial(_combine_kernel, tm=tm),
        grid_spec=grid_spec,
        out_shape=jax.ShapeDtypeStruct((n, d), F32),
        compiler_params=_cparams(("arbitrary",)),
        name="moe_combine",
    )(pos, ys, tg, x1, mod.arr, gpost.reshape(1, d))


def _with_prefetch(spec):
    inner = spec.index_map
    return pl.BlockSpec(spec.block_shape, lambda i, _s: inner(i))


def _route(top_i, n_experts, tm):
    n = top_i.shape[0]
    nk = n * TOP_K
    flat_e = top_i.reshape(-1)
    onehot = (flat_e[:, None] == jnp.arange(n_experts, dtype=jnp.int32)[None, :]).astype(jnp.int32)
    csum = jnp.cumsum(onehot, axis=0)
    counts = csum[-1]
    rank = jnp.sum(onehot * csum, axis=1) - 1
    pcounts = (counts + tm - 1) // tm * tm
    pend = jnp.cumsum(pcounts)
    pstart = pend - pcounts
    pos = jnp.sum(onehot * pstart[None, :], axis=1) + rank
    n_slots = (nk + n_experts * (tm - 1) + tm - 1) // tm * tm
    slot_tok = jnp.zeros((n_slots,), jnp.int32).at[pos].set(jnp.arange(nk, dtype=jnp.int32) // TOP_K)
    tiles = jnp.arange(n_slots // tm, dtype=jnp.int32) * tm
    tile_expert = jnp.minimum(jnp.searchsorted(pend, tiles, side='right'), n_experts - 1).astype(jnp.int32)
    n_valid = (pend[-1] // tm).astype(jnp.int32).reshape(1)
    return slot_tok, pos.astype(jnp.int32), tile_expert, n_valid


def _moe(h2, top_i, tg, x1, mod, gpost, w_up, b_up, w_dn, b_dn, tm_ffn):
    n_experts = w_up.shape[0]
    slot_tok, pos, tile_expert, n_valid = _route(top_i, n_experts, tm_ffn)
    xg = _gather_rows(h2, slot_tok, tm_ffn)
    ys = _expert_ffn(xg, tile_expert, n_valid, w_up, b_up, w_dn, b_dn, tm_ffn)
    return _combine(ys, pos, tg, x1, mod, gpost)


def _layer(x, mod, lw, attend, retain, d):
    tm = mod.tm
    cq, ck, cv = 0, d, 2 * d
    rq, rk, rv, rg = 3 * d, 4 * d, 5 * d, 7 * d
    ga, gb = 9 * d, 10 * d
    h = _prenorm(x, lw['g_pre1'], mod, 1, 0)
    dq, = _matmul(h, lw['w_in'], cq, d, (BF16,), tm)
    k32, k16 = _matmul(h, lw['w_in'], ck, d, (F32, BF16), tm)
    v32, v16 = _matmul(h, lw['w_in'], cv, d, (F32, BF16), tm)
    z, = _matmul(h, lw['w_in'], rq, 8 * d, (BF16,), tm)
    o_a = attend(dq, k16, v16)
    o_b, state = retain(z, rq - rq, rk - rq, rv - rq, rg - rq)
    x1, h2, top_i, tg = _postmix(o_a, o_b, z, ga - rq, gb - rq, x, mod,
                                 lw['w_a'], lw['w_b'], lw['w_o'], lw['g_post1'], lw['g_pre2'],
                                 lw['w_r'], lw['b_r'])
    y = _moe(h2, top_i[:, :TOP_K], tg, x1, mod, lw['g_post2'],
             lw['w_up'], lw['b_up'], lw['w_dn'], lw['b_dn'], lw['tm_ffn'])
    return y, k32, v32, state


def _pick(n, pref):
    t = min(n, pref)
    while n % t:
        t //= 2
    return t


def kernel(x_prompt, x_sample, cache_k, cache_v, state_ret, page_table, c_prompt, c_sample, w_ada, b_ada, norm_pre_mix, norm_post_mix, norm_pre_ffn, norm_post_ffn, w_in, lambda_q1, lambda_k1, lambda_q2, lambda_k2, diff_subln, w_branch_a, w_branch_b, w_out, w_router, b_router, w_exp_up, b_exp_up, w_exp_down, b_exp_down):
    batch, seq, d = x_prompt.shape
    nb, dec_seq, _ = x_sample.shape
    assert dec_seq == 1
    depth = w_ada.shape[0]
    n_phys = cache_k.shape[1]
    n_pages = page_table.shape[1]
    past = n_pages * PAGE_SIZE
    n_experts = w_router.shape[-1]
    dff = w_exp_down.shape[-2]
    dk = d // RET_HEADS

    inv = ROPE_BASE ** (-jnp.arange(dk // 2, dtype=F32) * 2.0 / dk)
    ang_p = jnp.arange(seq, dtype=jnp.int32).astype(F32)[:, None] * inv[None, :]
    ang_s = jnp.full((1, 1), past, jnp.int32).astype(F32) * inv[None, :]
    cos_p, sin_p, cos_s, sin_s = jnp.cos(ang_p), jnp.sin(ang_p), jnp.cos(ang_s), jnp.sin(ang_s)
    lg = jnp.log(1.0 - 2.0 ** (-5.0 - jnp.arange(RET_HEADS, dtype=F32)))

    cache_k2 = cache_k.reshape(depth * n_phys, PAGE_SIZE, d)
    cache_v2 = cache_v.reshape(depth * n_phys, PAGE_SIZE, d)

    tm_p = _pick(batch * seq, 256)
    t_attn = _pick(seq, 256)
    c_ret = _pick(seq, 256)
    pages_per_step = _pick(n_pages, 4)

    yp = x_prompt.reshape(batch * seq, d)
    ys = x_sample.reshape(nb, d)
    c_all = jnp.concatenate([c_prompt, c_sample], axis=0)
    pad = (-c_all.shape[0]) % 8
    c_all = jnp.pad(c_all, ((0, pad), (0, 0)))
    outs = [[] for _ in range(6)]
    for l in range(depth):
        lam_init = 0.8 - 0.6 * math.exp(-0.3 * l)
        lams = tuple(a[l].reshape(1, DIFF_HD) for a in (lambda_q1, lambda_k1, lambda_q2, lambda_k2))
        table = _ada(c_all, w_ada[l], b_ada[l])
        mod_p = _Mod(table[:batch], d, tm_p, seq)
        mod_s = _Mod(table[batch:batch + nb], d, nb, 1)
        w_up = w_exp_up[l].reshape(n_experts, d, dff, 2).transpose(0, 3, 1, 2).astype(BF16)
        b_up = b_exp_up[l].reshape(n_experts, dff, 2).transpose(0, 2, 1)
        w_r = jnp.pad(w_router[l], ((0, 0), (0, LANES - n_experts)))
        b_r = jnp.pad(b_router[l], (0, LANES - n_experts), constant_values=NEG).reshape(1, LANES)
        lw = dict(g_pre1=norm_pre_mix[l], g_post1=norm_post_mix[l], g_pre2=norm_pre_ffn[l],
                  g_post2=norm_post_ffn[l], w_in=w_in[l],
                  w_a=w_branch_a[l].astype(BF16), w_b=w_branch_b[l].astype(BF16),
                  w_o=w_out[l].astype(BF16), w_r=w_r, b_r=b_r,
                  w_up=w_up, b_up=b_up, w_dn=w_exp_down[l].astype(BF16), b_dn=b_exp_down[l])

        attend_p = lambda q, k, v: _attn_prompt(q, k, v, lams, diff_subln[l], batch, seq, lam_init, t_attn)
        retain_p = lambda z, a, b, c, g: _ret_prompt(z, a, b, c, g, batch, seq, cos_p, sin_p, lg, c_ret)
        yp, kp, vp, rp = _layer(yp, mod_p, dict(lw, tm_ffn=_pick(batch * seq, 256)), attend_p, retain_p, d)

        attend_s = lambda q, k, v: _attn_decode(q, k, v, cache_k2, cache_v2, l * n_phys, page_table,
                                                lams, diff_subln[l], lam_init, pages_per_step)
        retain_s = lambda z, a, b, c, g: _ret_decode(z, a, b, c, g, state_ret[l], cos_s, sin_s, lg)
        ys, ks, vs, rs = _layer(ys, mod_s, dict(lw, tm_ffn=16), attend_s, retain_s, d)

        for lst, val in zip(outs, (kp.reshape(batch, seq, DIFF_HEADS, HEAD_W),
                                   vp.reshape(batch, seq, DIFF_HEADS, HEAD_W),
                                   ks.reshape(nb, 1, DIFF_HEADS, HEAD_W),
                                   vs.reshape(nb, 1, DIFF_HEADS, HEAD_W), rp, rs)):
            lst.append(val)
    return (yp.reshape(batch, seq, d), ys.reshape(nb, 1, d)) + tuple(jnp.stack(o) for o in outs)
```

```python
import functools
import math

import jax
import jax.numpy as jnp
from jax import lax
from jax.experimental import pallas as pl
from jax.experimental.pallas import tpu as pltpu

F32 = jnp.float32
BF16 = jnp.bfloat16

DIFF_HEADS = 8
DIFF_HD = 64
HEAD_W = 2 * DIFF_HD
RET_HEADS = 4
PAGE_SIZE = 128
TOP_K = 4
ROPE_BASE = 10000.0
SWIGLU_LIMIT = 7.0
SWIGLU_ALPHA = 1.702
EPS = 1e-6
SUBLN_EPS = 1e-5
LANES = 128
VMEM_LIMIT = 52 * 1024 * 1024
NEG = -1e30

_NT = (((1,), (1,)), ((), ()))


def _cparams(sem):
    return pltpu.CompilerParams(dimension_semantics=sem, vmem_limit_bytes=VMEM_LIMIT)


def _sigmoid(x):
    return 1.0 / (1.0 + jnp.exp(-x))


def _rms(x, eps):
    return x * lax.rsqrt(jnp.mean(x * x, axis=-1, keepdims=True) + eps)


def _dot(a, b):
    return jnp.dot(a, b, preferred_element_type=F32)


def _ada_kernel(c_ref, w_ref, b_ref, o_ref):
    c = c_ref[...]
    s = (c * _sigmoid(c)).astype(BF16)
    o_ref[...] = _dot(s, w_ref[...].astype(BF16)) + b_ref[...]


def _ada(c, w, b, tn=1024):
    r, d = c.shape
    n = w.shape[1]
    return pl.pallas_call(
        _ada_kernel,
        grid=(n // tn,),
        in_specs=[pl.BlockSpec((r, d), lambda j: (0, 0)),
                  pl.BlockSpec((d, tn), lambda j: (0, j)),
                  pl.BlockSpec((1, tn), lambda j: (0, j))],
        out_specs=pl.BlockSpec((r, tn), lambda j: (0, j)),
        out_shape=jax.ShapeDtypeStruct((r, n), F32),
        compiler_params=_cparams(("arbitrary",)),
        name="ada_mod",
    )(c, w, b.reshape(1, n))


class _Mod:
    def __init__(self, table, d, tm, rows_per_entry):
        self.d = d
        self.tm = tm
        if rows_per_entry == 1:
            self.arr = table
            self.per_row = True
        else:
            self.arr = table.reshape(table.shape[0], 1, table.shape[1])
            self.per_row = False
            self.tiles_per_entry = rows_per_entry // tm

    def spec(self, c):
        if self.per_row:
            return pl.BlockSpec((self.tm, self.d), lambda i: (i, c))
        tpe = self.tiles_per_entry
        return pl.BlockSpec((None, 1, self.d), lambda i: (i // tpe, 0, c))


def _prenorm_kernel(x_ref, g_ref, sc_ref, sh_ref, o_ref):
    y = _rms(x_ref[...], EPS) * g_ref[...]
    o_ref[...] = (y * (1.0 + sc_ref[...]) + sh_ref[...]).astype(o_ref.dtype)


def _prenorm(x, g, mod, c_scale, c_shift):
    n, d = x.shape
    tm = mod.tm
    return pl.pallas_call(
        _prenorm_kernel,
        grid=(n // tm,),
        in_specs=[pl.BlockSpec((tm, d), lambda i: (i, 0)),
                  pl.BlockSpec((1, d), lambda i: (0, 0)),
                  mod.spec(c_scale), mod.spec(c_shift)],
        out_specs=pl.BlockSpec((tm, d), lambda i: (i, 0)),
        out_shape=jax.ShapeDtypeStruct((n, d), BF16),
        compiler_params=_cparams(("arbitrary",)),
        name="prenorm",
    )(x, g.reshape(1, d), mod.arr, mod.arr)


def _mm_kernel(h_ref, w_ref, *rest):
    outs, wb = rest[:-1], rest[-1]

    @pl.when(pl.program_id(1) == 0)
    def _():
        wb[...] = w_ref[...].astype(BF16)

    acc = _dot(h_ref[...], wb[...])
    for o in outs:
        o[...] = acc.astype(o.dtype)


def _matmul(h, w, col0, ncols, out_dtypes, tm, tn=1024):
    n, k = h.shape
    off = col0 // tn
    outs = pl.pallas_call(
        _mm_kernel,
        grid=(ncols // tn, n // tm),
        in_specs=[pl.BlockSpec((tm, k), lambda j, i: (i, 0)),
                  pl.BlockSpec((k, tn), lambda j, i: (0, j + off))],
        out_specs=[pl.BlockSpec((tm, tn), lambda j, i: (i, j)) for _ in out_dtypes],
        out_shape=[jax.ShapeDtypeStruct((n, ncols), dt) for dt in out_dtypes],
        scratch_shapes=[pltpu.VMEM((k, tn), BF16)],
        compiler_params=_cparams(("arbitrary", "arbitrary")),
        name="in_proj",
    )(h, w)
    return outs


def _lambda(lq1, lk1, lq2, lk2, lam_init):
    a = jnp.sum(lq1[...] * lk1[...], axis=-1, keepdims=True)
    b = jnp.sum(lq2[...] * lk2[...], axis=-1, keepdims=True)
    return jnp.exp(a) - jnp.exp(b) + lam_init


def _stack_components(q):
    lane = lax.broadcasted_iota(jnp.int32, q.shape, 1)
    zero = jnp.zeros_like(q)
    return jnp.concatenate([jnp.where(lane < DIFF_HD, q, zero),
                            jnp.where(lane >= DIFF_HD, q, zero)], axis=0)


def _attn_prompt_kernel(lq1, lk1, lq2, lk2, q_ref, k_ref, v_ref, sub_ref, o_ref,
                        qs, vext, m_s, acc, *, t, lam_init):
    qi = pl.program_id(2)

    @pl.when(qi == 0)
    def _():
        vext[:, :HEAD_W] = v_ref[...]
        vext[:, HEAD_W:] = jnp.ones((vext.shape[0], HEAD_W), vext.dtype)

    qs[...] = _stack_components(q_ref[...].astype(F32) * (DIFF_HD ** -0.5)).astype(BF16)
    m_s[...] = jnp.full(m_s.shape, -jnp.inf, F32)
    acc[...] = jnp.zeros(acc.shape, F32)

    def step(ki, masked):
        start = pl.multiple_of(ki * t, t)
        k = k_ref[pl.ds(start, t), :]
        v = vext[pl.ds(start, t), :]
        s = lax.dot_general(qs[...], k, _NT, preferred_element_type=F32)
        if masked:
            row = lax.broadcasted_iota(jnp.int32, s.shape, 0)
            col = lax.broadcasted_iota(jnp.int32, s.shape, 1)
            row = jnp.where(row >= t, row - t, row)
            s = jnp.where(col <= row, s, -jnp.inf)
        chunks = [s[:, j * LANES:(j + 1) * LANES] for j in range(t // LANES)]
        cmax = chunks[0]
        for c in chunks[1:]:
            cmax = jnp.maximum(cmax, c)
        m_old = m_s[...]
        m_new = jnp.maximum(m_old, jnp.broadcast_to(jnp.max(cmax, axis=-1, keepdims=True), m_old.shape))
        alpha = jnp.exp(m_old - m_new)
        p = jnp.concatenate([jnp.exp(c - m_new).astype(BF16) for c in chunks], axis=1)
        acc[...] = jnp.concatenate([alpha, alpha], axis=1) * acc[...] + _dot(p, v)
        m_s[...] = m_new

    def body(ki, carry):
        step(ki, False)
        return carry

    lax.fori_loop(0, qi, body, 0)
    step(qi, True)

    lam = _lambda(lq1, lk1, lq2, lk2, lam_init)
    a = acc[...]
    o = a[:, :HEAD_W] / a[:, HEAD_W:]
    o = o[:t] - lam * o[t:]
    o = _rms(o, SUBLN_EPS) * sub_ref[...] * (1.0 - lam_init)
    o_ref[...] = o.astype(o_ref.dtype)


def _attn_prompt(q, k, v, lams, subln, batch, seq, lam_init, t):
    n, w = q.shape
    nq = seq // t
    lam_specs = [pl.BlockSpec((1, DIFF_HD), lambda b, h, i: (0, 0)) for _ in range(4)]
    return pl.pallas_call(
        functools.partial(_attn_prompt_kernel, t=t, lam_init=lam_init),
        grid=(batch, DIFF_HEADS, nq),
        in_specs=lam_specs + [
            pl.BlockSpec((t, HEAD_W), lambda b, h, i: (b * nq + i, h)),
            pl.BlockSpec((seq, HEAD_W), lambda b, h, i: (b, h)),
            pl.BlockSpec((seq, HEAD_W), lambda b, h, i: (b, h)),
            pl.BlockSpec((1, HEAD_W), lambda b, h, i: (0, 0))],
        out_specs=pl.BlockSpec((t, HEAD_W), lambda b, h, i: (b * nq + i, h)),
        out_shape=jax.ShapeDtypeStruct((n, w), BF16),
        scratch_shapes=[pltpu.VMEM((2 * t, HEAD_W), BF16),
                        pltpu.VMEM((seq, 2 * HEAD_W), BF16),
                        pltpu.VMEM((2 * t, HEAD_W), F32),
                        pltpu.VMEM((2 * t, 2 * HEAD_W), F32)],
        compiler_params=_cparams(("arbitrary", "arbitrary", "arbitrary")),
        name="diff_attn_prompt",
    )(*lams, q, k, v, subln.reshape(1, HEAD_W))


def _attn_decode_kernel(pt_ref, lq1, lk1, lq2, lk2, q_ref, kn_ref, vn_ref, sub_ref, *rest,
                        pages_per_step, lam_init):
    del pt_ref
    k_refs = rest[:pages_per_step]
    v_refs = rest[pages_per_step:2 * pages_per_step]
    o_ref, qm, bias, m_s, l_s, acc = rest[2 * pages_per_step:]
    p_idx = pl.program_id(1)

    @pl.when(p_idx == 0)
    def _():
        qm[...] = _stack_components(q_ref[...] * (DIFF_HD ** -0.5)).astype(BF16)
        row = lax.broadcasted_iota(jnp.int32, bias.shape, 0)
        col = lax.broadcasted_iota(jnp.int32, bias.shape, 1)
        bias[...] = jnp.where(row % DIFF_HEADS == col % DIFF_HEADS, 0.0, -jnp.inf)
        m_s[...] = jnp.full(m_s.shape, -jnp.inf, F32)
        l_s[...] = jnp.zeros(l_s.shape, F32)
        acc[...] = jnp.zeros(acc.shape, F32)

    scores = [lax.dot_general(qm[...], k_ref[...].astype(BF16), _NT, preferred_element_type=F32)
              + bias[...] for k_ref in k_refs]
    m_new = m_s[...]
    for s in scores:
        m_new = jnp.maximum(m_new, jnp.max(s, axis=-1, keepdims=True))
    alpha = jnp.exp(m_s[...] - m_new)
    l_new = alpha * l_s[...]
    a_new = alpha * acc[...]
    for s, v_ref in zip(scores, v_refs):
        p = jnp.exp(s - m_new)
        l_new = l_new + jnp.sum(p, axis=-1, keepdims=True)
        a_new = a_new + _dot(p.astype(BF16), v_ref[...].astype(BF16))
    m_s[...] = m_new
    l_s[...] = l_new
    acc[...] = a_new

    @pl.when(p_idx == pl.num_programs(1) - 1)
    def _():
        kn = jnp.concatenate([kn_ref[...], kn_ref[...]], axis=0)
        vn = jnp.concatenate([vn_ref[...], vn_ref[...]], axis=0)
        s = jnp.sum(qm[...].astype(F32) * kn, axis=-1, keepdims=True)
        m_fin = jnp.maximum(m_new, s)
        beta = jnp.exp(m_new - m_fin)
        p = jnp.exp(s - m_fin)
        o = (beta * a_new + p * vn) / (beta * l_new + p)
        lam = _lambda(lq1, lk1, lq2, lk2, lam_init)
        o = o[:DIFF_HEADS] - lam * o[DIFF_HEADS:]
        o_ref[...] = _rms(o, SUBLN_EPS) * sub_ref[...] * (1.0 - lam_init)


def _attn_decode(q, k_new, v_new, cache_k, cache_v, page_base, page_table, lams, subln,
                 lam_init, pages_per_step):
    nb, d = q.shape
    n_pages = page_table.shape[1]
    g = pages_per_step
    steps = n_pages // g
    rows = PAGE_SIZE * DIFF_HEADS
    pt_flat = (page_table + page_base).reshape(-1).astype(jnp.int32)

    def page_spec(r):
        return pl.BlockSpec((None, rows, HEAD_W),
                            lambda b, p, pt: (pt[b * n_pages + p * g + r], 0, 0))

    head_spec = pl.BlockSpec((None, DIFF_HEADS, HEAD_W), lambda b, p, pt: (b, 0, 0))
    lam_specs = [pl.BlockSpec((1, DIFF_HD), lambda b, p, pt: (0, 0)) for _ in range(4)]
    grid_spec = pltpu.PrefetchScalarGridSpec(
        num_scalar_prefetch=1,
        grid=(nb, steps),
        in_specs=lam_specs + [head_spec, head_spec, head_spec,
                              pl.BlockSpec((1, HEAD_W), lambda b, p, pt: (0, 0))]
        + [page_spec(r) for r in range(g)] + [page_spec(r) for r in range(g)],
        out_specs=head_spec,
        scratch_shapes=[pltpu.VMEM((2 * DIFF_HEADS, HEAD_W), BF16),
                        pltpu.VMEM((2 * DIFF_HEADS, rows), F32),
                        pltpu.VMEM((2 * DIFF_HEADS, 1), F32),
                        pltpu.VMEM((2 * DIFF_HEADS, 1), F32),
                        pltpu.VMEM((2 * DIFF_HEADS, HEAD_W), F32)])
    per_head = lambda a: a.astype(F32).reshape(nb, DIFF_HEADS, HEAD_W)
    out = pl.pallas_call(
        functools.partial(_attn_decode_kernel, pages_per_step=g, lam_init=lam_init),
        grid_spec=grid_spec,
        out_shape=jax.ShapeDtypeStruct((nb, DIFF_HEADS, HEAD_W), F32),
        compiler_params=_cparams(("arbitrary", "arbitrary")),
        name="diff_attn_decode",
    )(pt_flat, *lams, per_head(q), per_head(k_new), per_head(v_new),
      subln.reshape(1, HEAD_W), *([cache_k] * g), *([cache_v] * g))
    return out.reshape(nb, d).astype(BF16)


def _rotate(x, cos, sin):
    half = x.shape[-1] // 2
    x1, x2 = x[:, :half], x[:, half:]
    return jnp.concatenate([x1 * cos - x2 * sin, x2 * cos + x1 * sin], axis=-1)


def _ret_prompt_kernel(lg_ref, q_ref, k_ref, v_ref, g_ref, cos_ref, sin_ref, o_ref, st_ref,
                       state, dmask, *, c):
    h = pl.program_id(1)
    ci = pl.program_id(2)
    lg = lg_ref[h]
    idx = lax.broadcasted_iota(jnp.int32, (c, 1), 0).astype(F32)

    @pl.when(ci == 0)
    def _():
        state[...] = jnp.zeros(state.shape, F32)
        row = lax.broadcasted_iota(jnp.int32, (c, c), 0)
        col = lax.broadcasted_iota(jnp.int32, (c, c), 1)
        diff = (row - col).astype(F32)
        dmask[...] = jnp.where(diff >= 0, jnp.exp(jnp.maximum(diff, 0.0) * lg), 0.0)

    cos, sin = cos_ref[...], sin_ref[...]
    q = _rotate(q_ref[...].astype(F32), cos, sin)
    k = _rotate(k_ref[...].astype(F32), cos, sin) * (q.shape[-1] ** -0.5)
    v = v_ref[...]
    q_dec = jnp.exp((idx + 1.0) * lg)
    k_dec = jnp.exp((c - 1.0 - idx) * lg)
    c_dec = jnp.exp(jnp.full((1, 1), c * lg, F32))

    s = lax.dot_general(q.astype(BF16), k.astype(BF16), _NT, preferred_element_type=F32) * dmask[...]
    st = state[...]
    o = _dot(s.astype(BF16), v) + _dot((q * q_dec).astype(BF16), st.astype(BF16))
    kd_t = (k * k_dec).T.astype(BF16)
    st = st * c_dec + _dot(kd_t, v)
    state[...] = st

    o = _rms(o, EPS)
    gate = g_ref[...].astype(F32)
    o_ref[...] = (o * (gate * _sigmoid(gate))).astype(o_ref.dtype)

    @pl.when(ci == pl.num_programs(2) - 1)
    def _():
        st_ref[...] = st


def _ret_prompt(z, col_q, col_k, col_v, col_g, batch, seq, cos, sin, lg, c):
    n = z.shape[0]
    dk = (col_k - col_q) // RET_HEADS
    dv = (col_g - col_v) // RET_HEADS
    nc = seq // c
    qo, ko, vo, go = col_q // dk, col_k // dk, col_v // dv, col_g // dv
    grid_spec = pltpu.PrefetchScalarGridSpec(
        num_scalar_prefetch=0,
        grid=(batch, RET_HEADS, nc),
        in_specs=[pl.BlockSpec(memory_space=pltpu.SMEM),
                  pl.BlockSpec((c, dk), lambda b, h, i: (b * nc + i, qo + h)),
                  pl.BlockSpec((c, dk), lambda b, h, i: (b * nc + i, ko + h)),
                  pl.BlockSpec((c, dv), lambda b, h, i: (b * nc + i, vo + h)),
                  pl.BlockSpec((c, dv), lambda b, h, i: (b * nc + i, go + h)),
                  pl.BlockSpec((c, dk // 2), lambda b, h, i: (i, 0)),
                  pl.BlockSpec((c, dk // 2), lambda b, h, i: (i, 0))],
        out_specs=[pl.BlockSpec((c, dv), lambda b, h, i: (b * nc + i, h)),
                   pl.BlockSpec((None, None, dk, dv), lambda b, h, i: (b, h, 0, 0))],
        scratch_shapes=[pltpu.VMEM((dk, dv), F32), pltpu.VMEM((c, c), F32)])
    return pl.pallas_call(
        functools.partial(_ret_prompt_kernel, c=c),
        grid_spec=grid_spec,
        out_shape=[jax.ShapeDtypeStruct((n, RET_HEADS * dv), BF16),
                   jax.ShapeDtypeStruct((batch, RET_HEADS, dk, dv), F32)],
        compiler_params=_cparams(("arbitrary", "arbitrary", "arbitrary")),
        name="retention_prompt",
    )(lg, z, z, z, z, cos, sin)


def _ret_decode_kernel(lg_ref, q_ref, k_ref, v_ref, g_ref, cos_ref, sin_ref, st_ref, o_ref, so_ref):
    dk, dv = st_ref.shape[1], st_ref.shape[2]
    cos, sin = cos_ref[...], sin_ref[...]
    for h in range(RET_HEADS):
        gamma = jnp.exp(jnp.full((1, 1), lg_ref[h], F32))
        q = _rotate(q_ref[:, h * dk:(h + 1) * dk].astype(F32), cos, sin)
        k = _rotate(k_ref[:, h * dk:(h + 1) * dk].astype(F32), cos, sin) * (dk ** -0.5)
        v = v_ref[:, h * dv:(h + 1) * dv].astype(F32)
        st = st_ref[h]
        q_col = jnp.broadcast_to(q * gamma, (LANES, dk)).T[:, 0:1]
        k_col = jnp.broadcast_to(k, (LANES, dk)).T[:, 0:1]
        qk = jnp.sum(q * k, axis=-1, keepdims=True)
        o = qk * v + jnp.sum(q_col * st, axis=0, keepdims=True)
        so_ref[h] = st * gamma + k_col * v
        o = _rms(o, EPS)
        gate = g_ref[:, h * dv:(h + 1) * dv].astype(F32)
        o_ref[:, h * dv:(h + 1) * dv] = (o * (gate * _sigmoid(gate))).astype(o_ref.dtype)


def _ret_decode(z, col_q, col_k, col_v, col_g, state, cos, sin, lg):
    nb = z.shape[0]
    _, _, dk, dv = state.shape
    z3 = z.reshape(nb, 1, z.shape[1])
    wq, wv = RET_HEADS * dk, RET_HEADS * dv

    def row(width, col):
        return pl.BlockSpec((None, 1, width), lambda b: (b, 0, col // width))

    o, st = pl.pallas_call(
        _ret_decode_kernel,
        grid=(nb,),
        in_specs=[pl.BlockSpec(memory_space=pltpu.SMEM),
                  row(wq, col_q), row(wq, col_k), row(wv, col_v), row(wv, col_g),
                  pl.BlockSpec((1, dk // 2), lambda b: (0, 0)),
                  pl.BlockSpec((1, dk // 2), lambda b: (0, 0)),
                  pl.BlockSpec((None, RET_HEADS, dk, dv), lambda b: (b, 0, 0, 0))],
        out_specs=[pl.BlockSpec((None, 1, wv), lambda b: (b, 0, 0)),
                   pl.BlockSpec((None, RET_HEADS, dk, dv), lambda b: (b, 0, 0, 0))],
        out_shape=[jax.ShapeDtypeStruct((nb, 1, wv), BF16),
                   jax.ShapeDtypeStruct(state.shape, F32)],
        compiler_params=_cparams(("arbitrary",)),
        name="retention_decode",
    )(lg, z3, z3, z3, z3, cos, sin, state)
    return o.reshape(nb, wv), st


def _postmix_kernel(oa_ref, ob_ref, ga_ref, gb_ref, x_ref, gt1_ref, sc2_ref, sh2_ref,
                    wa_ref, wb_ref, wo_ref, gpost_ref, gpre_ref, wr_ref, br_ref,
                    x1_ref, h2_ref, ti_ref, tg_ref):
    a = _dot(oa_ref[...], wa_ref[...])
    b = _dot(ob_ref[...], wb_ref[...])
    merged = _sigmoid(ga_ref[...].astype(F32)) * a + _sigmoid(gb_ref[...].astype(F32)) * b
    y = _dot(merged.astype(BF16), wo_ref[...])
    x1 = x_ref[...] + gt1_ref[...] * (_rms(y, EPS) * gpost_ref[...])
    x1_ref[...] = x1
    h2 = _rms(x1, EPS) * gpre_ref[...] * (1.0 + sc2_ref[...]) + sh2_ref[...]
    h2_ref[...] = h2

    logits = jnp.dot(h2, wr_ref[...], preferred_element_type=F32,
                     precision=lax.Precision.HIGHEST) + br_ref[...]
    lane = lax.broadcasted_iota(jnp.int32, logits.shape, 1)
    cur = logits
    vals, idxs = [], []
    for _ in range(TOP_K):
        m = jnp.max(cur, axis=-1, keepdims=True)
        idx = jnp.min(jnp.where(cur == m, lane, LANES), axis=-1, keepdims=True)
        vals.append(m)
        idxs.append(idx)
        cur = jnp.where(lane == idx, NEG * 2.0, cur)
    exps = [jnp.exp(v - vals[0]) for v in vals]
    denom = exps[0]
    for e in exps[1:]:
        denom = denom + e
    ti = jnp.zeros(logits.shape, jnp.int32)
    tg = jnp.zeros(logits.shape, F32)
    for kk in range(TOP_K):
        ti = jnp.where(lane == kk, idxs[kk], ti)
        tg = jnp.where(lane == kk, exps[kk] / denom, tg)
    ti_ref[...] = ti
    tg_ref[...] = tg


def _postmix(o_a, o_b, z, col_ga, col_gb, x, mod, wa, wb, wo, gpost, gpre, wr, br):
    n, d = x.shape
    tm = mod.tm
    da, db = o_a.shape[1], o_b.shape[1]
    const = lambda shape: pl.BlockSpec(shape, lambda i: (0, 0))
    rows = lambda w: pl.BlockSpec((tm, w), lambda i: (i, 0))
    return pl.pallas_call(
        _postmix_kernel,
        grid=(n // tm,),
        in_specs=[rows(da), rows(db),
                  pl.BlockSpec((tm, d), lambda i: (i, col_ga // d)),
                  pl.BlockSpec((tm, d), lambda i: (i, col_gb // d)),
                  rows(d), mod.spec(2), mod.spec(4), mod.spec(3),
                  const((da, d)), const((db, d)), const((d, d)),
                  const((1, d)), const((1, d)), const((d, LANES)), const((1, LANES))],
        out_specs=[rows(d), rows(d), rows(LANES), rows(LANES)],
        out_shape=[jax.ShapeDtypeStruct((n, d), F32), jax.ShapeDtypeStruct((n, d), F32),
                   jax.ShapeDtypeStruct((n, LANES), jnp.int32),
                   jax.ShapeDtypeStruct((n, LANES), F32)],
        compiler_params=_cparams(("arbitrary",)),
        name="postmix_router",
    )(o_a, o_b, z, z, x, mod.arr, mod.arr, mod.arr, wa, wb, wo,
      gpost.reshape(1, d), gpre.reshape(1, d), wr, br)


def _row_copy(src_hbm, row, dst, r, sem):
    return pltpu.make_async_copy(src_hbm.at[pl.ds(row, 1)], dst.at[pl.ds(r, 1)], sem)


def _gather_kernel(tok_ref, h_hbm, o_ref, sem, *, tm):
    base = pl.program_id(0) * tm

    def issue(r, carry):
        _row_copy(h_hbm, tok_ref[base + r], o_ref, r, sem).start()
        return carry

    lax.fori_loop(0, tm, issue, 0, unroll=16)

    def wait(r, carry):
        _row_copy(h_hbm, 0, o_ref, r, sem).wait()
        return carry

    lax.fori_loop(0, tm, wait, 0, unroll=16)


def _gather_rows(h, slot_tok, tm):
    p = slot_tok.shape[0]
    d = h.shape[1]
    grid_spec = pltpu.PrefetchScalarGridSpec(
        num_scalar_prefetch=1,
        grid=(p // tm,),
        in_specs=[pl.BlockSpec(memory_space=pl.ANY)],
        out_specs=pl.BlockSpec((tm, d), lambda t, tok: (t, 0)),
        scratch_shapes=[pltpu.SemaphoreType.DMA(())])
    return pl.pallas_call(
        functools.partial(_gather_kernel, tm=tm),
        grid_spec=grid_spec,
        out_shape=jax.ShapeDtypeStruct((p, d), h.dtype),
        compiler_params=_cparams(("arbitrary",)),
        name="moe_gather",
    )(slot_tok, h)


def _ffn_kernel(te_ref, nv_ref, x_ref, wg_ref, wl_ref, bg_ref, bl_ref, wd_ref, bd_ref, o_ref):
    del te_ref
    t = pl.program_id(0)

    @pl.when(t < nv_ref[0])
    def _():
        x = x_ref[...].astype(BF16)
        g = jnp.minimum(_dot(x, wg_ref[...]) + bg_ref[...], SWIGLU_LIMIT)
        l = jnp.clip(_dot(x, wl_ref[...]) + bl_ref[...], -SWIGLU_LIMIT, SWIGLU_LIMIT)
        act = g * _sigmoid(SWIGLU_ALPHA * g) * (l + 1.0)
        o_ref[...] = _dot(act.astype(BF16), wd_ref[...]) + bd_ref[...]

    @pl.when(t >= nv_ref[0])
    def _():
        o_ref[...] = jnp.zeros(o_ref.shape, o_ref.dtype)


def _expert_ffn(xg, tile_expert, n_valid, w_up, b_up, w_dn, b_dn, tm):
    p, d = xg.shape
    dff = w_up.shape[-1]
    e = w_up.shape[0]
    grid_spec = pltpu.PrefetchScalarGridSpec(
        num_scalar_prefetch=2,
        grid=(p // tm,),
        in_specs=[pl.BlockSpec((tm, d), lambda t, te, nv: (t, 0)),
                  pl.BlockSpec((None, None, d, dff), lambda t, te, nv: (te[t], 0, 0, 0)),
                  pl.BlockSpec((None, None, d, dff), lambda t, te, nv: (te[t], 1, 0, 0)),
                  pl.BlockSpec((None, None, 1, dff), lambda t, te, nv: (te[t], 0, 0, 0)),
                  pl.BlockSpec((None, None, 1, dff), lambda t, te, nv: (te[t], 1, 0, 0)),
                  pl.BlockSpec((None, dff, d), lambda t, te, nv: (te[t], 0, 0)),
                  pl.BlockSpec((None, 1, d), lambda t, te, nv: (te[t], 0, 0))],
        out_specs=pl.BlockSpec((tm, d), lambda t, te, nv: (t, 0)))
    return pl.pallas_call(
        _ffn_kernel,
        grid_spec=grid_spec,
        out_shape=jax.ShapeDtypeStruct((p, d), F32),
        compiler_params=_cparams(("arbitrary",)),
        name="moe_expert_ffn",
    )(tile_expert, n_valid, xg, w_up, w_up, b_up.reshape(e, 2, 1, dff), b_up.reshape(e, 2, 1, dff),
      w_dn, b_dn.reshape(e, 1, d))


def _combine_kernel(pos_ref, y_hbm, tg_ref, x1_ref, gt2_ref, gpost_ref, o_ref, buf, sem, *, tm):
    base = pl.program_id(0) * tm * TOP_K

    def issue(r, carry):
        for kk in range(TOP_K):
            _row_copy(y_hbm, pos_ref[base + r * TOP_K + kk], buf.at[kk], r, sem).start()
        return carry

    lax.fori_loop(0, tm, issue, 0, unroll=4)

    def wait(r, carry):
        for kk in range(TOP_K):
            _row_copy(y_hbm, 0, buf.at[kk], r, sem).wait()
        return carry

    lax.fori_loop(0, tm, wait, 0, unroll=4)

    tg = tg_ref[...]
    f = tg[:, 0:1] * buf[0]
    for kk in range(1, TOP_K):
        f = f + tg[:, kk:kk + 1] * buf[kk]
    o_ref[...] = x1_ref[...] + gt2_ref[...] * (_rms(f, EPS) * gpost_ref[...])


def _combine(ys, pos, tg, x1, mod, gpost):
    n, d = x1.shape
    tm = mod.tm
    grid_spec = pltpu.PrefetchScalarGridSpec(
        num_scalar_prefetch=1,
        grid=(n // tm,),
        in_specs=[pl.BlockSpec(memory_space=pl.ANY),
                  pl.BlockSpec((tm, LANES), lambda i, pos: (i, 0)),
                  pl.BlockSpec((tm, d), lambda i, pos: (i, 0)),
                  _with_prefetch(mod.spec(5)),
                  pl.BlockSpec((1, d), lambda i, pos: (0, 0))],
        out_specs=pl.BlockSpec((tm, d), lambda i, pos: (i, 0)),
        scratch_shapes=[pltpu.VMEM((TOP_K, tm, d), F32), pltpu.SemaphoreType.DMA(())])
    return pl.pallas_call(
        functools.partial(_combine_kernel, tm=tm),
        grid_spec=grid_spec,
        out_shape=jax.ShapeDtypeStruct((n, d), F32),
        compiler_params=_cparams(("arbitrary",)),
        name="moe_combine",
    )(pos, ys, tg, x1, mod.arr, gpost.reshape(1, d))


def _with_prefetch(spec):
    inner = spec.index_map
    return pl.BlockSpec(spec.block_shape, lambda i, _s: inner(i))


def _route(top_i, n_experts, tm):
    n = top_i.shape[0]
    nk = n * TOP_K
    flat_e = top_i.reshape(-1)
    onehot = (flat_e[:, None] == jnp.arange(n_experts, dtype=jnp.int32)[None, :]).astype(jnp.int32)
    csum = jnp.cumsum(onehot, axis=0)
    counts = csum[-1]
    rank = jnp.sum(onehot * csum, axis=1) - 1
    pcounts = (counts + tm - 1) // tm * tm
    pend = jnp.cumsum(pcounts)
    pstart = pend - pcounts
    pos = jnp.sum(onehot * pstart[None, :], axis=1) + rank
    n_slots = (nk + n_experts * (tm - 1) + tm - 1) // tm * tm
    slot_tok = jnp.zeros((n_slots,), jnp.int32).at[pos].set(jnp.arange(nk, dtype=jnp.int32) // TOP_K)
    tiles = jnp.arange(n_slots // tm, dtype=jnp.int32) * tm
    tile_expert = jnp.minimum(jnp.searchsorted(pend, tiles, side='right'), n_experts - 1).astype(jnp.int32)
    n_valid = (pend[-1] // tm).astype(jnp.int32).reshape(1)
    return slot_tok, pos.astype(jnp.int32), tile_expert, n_valid


def _moe(h2, top_i, tg, x1, mod, gpost, w_up, b_up, w_dn, b_dn, tm_ffn):
    n_experts = w_up.shape[0]
    slot_tok, pos, tile_expert, n_valid = _route(top_i, n_experts, tm_ffn)
    xg = _gather_rows(h2, slot_tok, _pick(slot_tok.shape[0], 4 * tm_ffn))
    ys = _expert_ffn(xg, tile_expert, n_valid, w_up, b_up, w_dn, b_dn, tm_ffn)
    return _combine(ys, pos, tg, x1, mod, gpost)


def _layer(x, mod, lw, attend, retain, d):
    tm = mod.tm
    cq, ck, cv = 0, d, 2 * d
    rq, rk, rv, rg = 3 * d, 4 * d, 5 * d, 7 * d
    ga, gb = 9 * d, 10 * d
    h = _prenorm(x, lw['g_pre1'], mod, 1, 0)
    tm_mm = _pick(x.shape[0], 512)
    dq, = _matmul(h, lw['w_in'], cq, d, (BF16,), tm_mm)
    k32, k16 = _matmul(h, lw['w_in'], ck, d, (F32, BF16), tm_mm)
    v32, v16 = _matmul(h, lw['w_in'], cv, d, (F32, BF16), tm_mm)
    z, = _matmul(h, lw['w_in'], rq, 8 * d, (BF16,), tm_mm)
    o_a = attend(dq, k16, v16)
    o_b, state = retain(z, rq - rq, rk - rq, rv - rq, rg - rq)
    x1, h2, top_i, tg = _postmix(o_a, o_b, z, ga - rq, gb - rq, x, mod,
                                 lw['w_a'], lw['w_b'], lw['w_o'], lw['g_post1'], lw['g_pre2'],
                                 lw['w_r'], lw['b_r'])
    y = _moe(h2, top_i[:, :TOP_K], tg, x1, mod, lw['g_post2'],
             lw['w_up'], lw['b_up'], lw['w_dn'], lw['b_dn'], lw['tm_ffn'])
    return y, k32, v32, state


def _pick(n, pref):
    t = min(n, pref)
    while n % t:
        t //= 2
    return t


def kernel(x_prompt, x_sample, cache_k, cache_v, state_ret, page_table, c_prompt, c_sample, w_ada, b_ada, norm_pre_mix, norm_post_mix, norm_pre_ffn, norm_post_ffn, w_in, lambda_q1, lambda_k1, lambda_q2, lambda_k2, diff_subln, w_branch_a, w_branch_b, w_out, w_router, b_router, w_exp_up, b_exp_up, w_exp_down, b_exp_down):
    batch, seq, d = x_prompt.shape
    nb, dec_seq, _ = x_sample.shape
    assert dec_seq == 1
    depth = w_ada.shape[0]
    n_phys = cache_k.shape[1]
    n_pages = page_table.shape[1]
    past = n_pages * PAGE_SIZE
    n_experts = w_router.shape[-1]
    dff = w_exp_down.shape[-2]
    dk = d // RET_HEADS

    inv = ROPE_BASE ** (-jnp.arange(dk // 2, dtype=F32) * 2.0 / dk)
    ang_p = jnp.arange(seq, dtype=jnp.int32).astype(F32)[:, None] * inv[None, :]
    ang_s = jnp.full((1, 1), past, jnp.int32).astype(F32) * inv[None, :]
    cos_p, sin_p, cos_s, sin_s = jnp.cos(ang_p), jnp.sin(ang_p), jnp.cos(ang_s), jnp.sin(ang_s)
    lg = jnp.log(1.0 - 2.0 ** (-5.0 - jnp.arange(RET_HEADS, dtype=F32)))

    cache_k2 = cache_k.reshape(depth * n_phys, PAGE_SIZE * DIFF_HEADS, HEAD_W)
    cache_v2 = cache_v.reshape(depth * n_phys, PAGE_SIZE * DIFF_HEADS, HEAD_W)

    tm_p = _pick(batch * seq, 256)
    t_attn = _pick(seq, 512)
    c_ret = _pick(seq, 256)
    pages_per_step = _pick(n_pages, 4)

    yp = x_prompt.reshape(batch * seq, d)
    ys = x_sample.reshape(nb, d)
    c_all = jnp.concatenate([c_prompt, c_sample], axis=0)
    pad = (-c_all.shape[0]) % 8
    c_all = jnp.pad(c_all, ((0, pad), (0, 0)))
    outs = [[] for _ in range(6)]
    for l in range(depth):
        lam_init = 0.8 - 0.6 * math.exp(-0.3 * l)
        lams = tuple(a[l].reshape(1, DIFF_HD) for a in (lambda_q1, lambda_k1, lambda_q2, lambda_k2))
        table = _ada(c_all, w_ada[l], b_ada[l])
        mod_p = _Mod(table[:batch], d, tm_p, seq)
        mod_s = _Mod(table[batch:batch + nb], d, nb, 1)
        w_up = w_exp_up[l].reshape(n_experts, d, dff, 2).transpose(0, 3, 1, 2).astype(BF16)
        b_up = b_exp_up[l].reshape(n_experts, dff, 2).transpose(0, 2, 1)
        w_r = jnp.pad(w_router[l], ((0, 0), (0, LANES - n_experts)))
        b_r = jnp.pad(b_router[l], (0, LANES - n_experts), constant_values=NEG).reshape(1, LANES)
        lw = dict(g_pre1=norm_pre_mix[l], g_post1=norm_post_mix[l], g_pre2=norm_pre_ffn[l],
                  g_post2=norm_post_ffn[l], w_in=w_in[l],
                  w_a=w_branch_a[l].astype(BF16), w_b=w_branch_b[l].astype(BF16),
                  w_o=w_out[l].astype(BF16), w_r=w_r, b_r=b_r,
                  w_up=w_up, b_up=b_up, w_dn=w_exp_down[l].astype(BF16), b_dn=b_exp_down[l])

        attend_p = lambda q, k, v: _attn_prompt(q, k, v, lams, diff_subln[l], batch, seq, lam_init, t_attn)
        retain_p = lambda z, a, b, c, g: _ret_prompt(z, a, b, c, g, batch, seq, cos_p, sin_p, lg, c_ret)
        yp, kp, vp, rp = _layer(yp, mod_p, dict(lw, tm_ffn=_pick(batch * seq, 256)), attend_p, retain_p, d)

        attend_s = lambda q, k, v: _attn_decode(q, k, v, cache_k2, cache_v2, l * n_phys, page_table,
                                                lams, diff_subln[l], lam_init, pages_per_step)
        retain_s = lambda z, a, b, c, g: _ret_decode(z, a, b, c, g, state_ret[l], cos_s, sin_s, lg)
        ys, ks, vs, rs = _layer(ys, mod_s, dict(lw, tm_ffn=16), attend_s, retain_s, d)

        for lst, val in zip(outs, (kp.reshape(batch, seq, DIFF_HEADS, HEAD_W),
                                   vp.reshape(batch, seq, DIFF_HEADS, HEAD_W),
                                   ks.reshape(nb, 1, DIFF_HEADS, HEAD_W),
                                   vs.reshape(nb, 1, DIFF_HEADS, HEAD_W), rp, rs)):
            lst.append(val)
    return (yp.reshape(batch, seq, d), ys.reshape(nb, 1, d)) + tuple(jnp.stack(o) for o in outs)
```

```python
import functools
import math

import jax
import jax.numpy as jnp
from jax import lax
from jax.experimental import pallas as pl
from jax.experimental.pallas import tpu as pltpu

F32 = jnp.float32
BF16 = jnp.bfloat16

DIFF_HEADS = 8
DIFF_HD = 64
HEAD_W = 2 * DIFF_HD
RET_HEADS = 4
PAGE_SIZE = 128
TOP_K = 4
ROPE_BASE = 10000.0
SWIGLU_LIMIT = 7.0
SWIGLU_ALPHA = 1.702
EPS = 1e-6
SUBLN_EPS = 1e-5
LANES = 128
VMEM_LIMIT = 52 * 1024 * 1024
NEG = -1e30
ATTN_ROW_GROUP = 256

_NT = (((1,), (1,)), ((), ()))


def _cparams(sem):
    return pltpu.CompilerParams(dimension_semantics=sem, vmem_limit_bytes=VMEM_LIMIT)


def _sigmoid(x):
    return 1.0 / (1.0 + jnp.exp(-x))


def _rms(x, eps):
    return x * lax.rsqrt(jnp.mean(x * x, axis=-1, keepdims=True) + eps)


def _dot(a, b):
    return jnp.dot(a, b, preferred_element_type=F32)


def _ada_kernel(c_ref, w_ref, b_ref, o_ref):
    c = c_ref[...]
    s = (c * _sigmoid(c)).astype(BF16)
    o_ref[...] = _dot(s, w_ref[...].astype(BF16)) + b_ref[...]


def _ada(c, w, b, tn=1024):
    r, d = c.shape
    n = w.shape[1]
    return pl.pallas_call(
        _ada_kernel,
        grid=(n // tn,),
        in_specs=[pl.BlockSpec((r, d), lambda j: (0, 0)),
                  pl.BlockSpec((d, tn), lambda j: (0, j)),
                  pl.BlockSpec((1, tn), lambda j: (0, j))],
        out_specs=pl.BlockSpec((r, tn), lambda j: (0, j)),
        out_shape=jax.ShapeDtypeStruct((r, n), F32),
        compiler_params=_cparams(("arbitrary",)),
        name="ada_mod",
    )(c, w, b.reshape(1, n))


class _Mod:
    def __init__(self, table, d, tm, rows_per_entry):
        self.d = d
        self.tm = tm
        if rows_per_entry == 1:
            self.arr = table
            self.per_row = True
        else:
            self.arr = table.reshape(table.shape[0], 1, table.shape[1])
            self.per_row = False
            self.tiles_per_entry = rows_per_entry // tm

    def spec(self, c):
        if self.per_row:
            return pl.BlockSpec((self.tm, self.d), lambda i: (i, c))
        tpe = self.tiles_per_entry
        return pl.BlockSpec((None, 1, self.d), lambda i: (i // tpe, 0, c))


def _prenorm_kernel(x_ref, g_ref, sc_ref, sh_ref, o_ref):
    y = _rms(x_ref[...], EPS) * g_ref[...]
    o_ref[...] = (y * (1.0 + sc_ref[...]) + sh_ref[...]).astype(o_ref.dtype)


def _prenorm(x, g, mod, c_scale, c_shift):
    n, d = x.shape
    tm = mod.tm
    return pl.pallas_call(
        _prenorm_kernel,
        grid=(n // tm,),
        in_specs=[pl.BlockSpec((tm, d), lambda i: (i, 0)),
                  pl.BlockSpec((1, d), lambda i: (0, 0)),
                  mod.spec(c_scale), mod.spec(c_shift)],
        out_specs=pl.BlockSpec((tm, d), lambda i: (i, 0)),
        out_shape=jax.ShapeDtypeStruct((n, d), BF16),
        compiler_params=_cparams(("arbitrary",)),
        name="prenorm",
    )(x, g.reshape(1, d), mod.arr, mod.arr)


def _mm_kernel(h_ref, w_ref, *rest):
    outs, wb = rest[:-1], rest[-1]

    @pl.when(pl.program_id(1) == 0)
    def _():
        wb[...] = w_ref[...].astype(BF16)

    acc = _dot(h_ref[...], wb[...])
    for o in outs:
        o[...] = acc.astype(o.dtype)


def _matmul(h, w, col0, ncols, out_dtypes, tm, tn=1024):
    n, k = h.shape
    off = col0 // tn
    outs = pl.pallas_call(
        _mm_kernel,
        grid=(ncols // tn, n // tm),
        in_specs=[pl.BlockSpec((tm, k), lambda j, i: (i, 0)),
                  pl.BlockSpec((k, tn), lambda j, i: (0, j + off))],
        out_specs=[pl.BlockSpec((tm, tn), lambda j, i: (i, j)) for _ in out_dtypes],
        out_shape=[jax.ShapeDtypeStruct((n, ncols), dt) for dt in out_dtypes],
        scratch_shapes=[pltpu.VMEM((k, tn), BF16)],
        compiler_params=_cparams(("arbitrary", "arbitrary")),
        name="in_proj",
    )(h, w)
    return outs


def _lambda(lq1, lk1, lq2, lk2, lam_init):
    a = jnp.sum(lq1[...] * lk1[...], axis=-1, keepdims=True)
    b = jnp.sum(lq2[...] * lk2[...], axis=-1, keepdims=True)
    return jnp.exp(a) - jnp.exp(b) + lam_init


def _stack_components(q):
    lane = lax.broadcasted_iota(jnp.int32, q.shape, 1)
    zero = jnp.zeros_like(q)
    return jnp.concatenate([jnp.where(lane < DIFF_HD, q, zero),
                            jnp.where(lane >= DIFF_HD, q, zero)], axis=0)


def _attn_prompt_kernel(lq1, lk1, lq2, lk2, q_ref, k_ref, v_ref, sub_ref, o_ref,
                        qs, vext, m_s, acc, *, t, lam_init):
    qi = pl.program_id(2)

    @pl.when(qi == 0)
    def _():
        vext[:, :HEAD_W] = v_ref[...]
        vext[:, HEAD_W:] = jnp.ones((vext.shape[0], HEAD_W), vext.dtype)

    qs[...] = _stack_components(q_ref[...].astype(F32) * (DIFF_HD ** -0.5)).astype(BF16)
    m_s[...] = jnp.full(m_s.shape, -jnp.inf, F32)
    acc[...] = jnp.zeros(acc.shape, F32)

    def step(ki, masked):
        start = pl.multiple_of(ki * t, t)
        k = k_ref[pl.ds(start, t), :]
        v = vext[pl.ds(start, t), :]
        group = min(ATTN_ROW_GROUP, t)
        for g0 in range(0, 2 * t, group):
            rows = pl.ds(g0, group)
            s = lax.dot_general(qs[rows, :], k, _NT, preferred_element_type=F32)
            if masked:
                row = lax.broadcasted_iota(jnp.int32, s.shape, 0) + (g0 % t)
                col = lax.broadcasted_iota(jnp.int32, s.shape, 1)
                s = jnp.where(col <= row, s, -jnp.inf)
            chunks = [s[:, j * LANES:(j + 1) * LANES] for j in range(t // LANES)]
            cmax = chunks[0]
            for c in chunks[1:]:
                cmax = jnp.maximum(cmax, c)
            m_old = m_s[rows, :]
            m_new = jnp.maximum(m_old, jnp.broadcast_to(jnp.max(cmax, axis=-1, keepdims=True), m_old.shape))
            alpha = jnp.exp(m_old - m_new)
            p = jnp.concatenate([jnp.exp(c - m_new).astype(BF16) for c in chunks], axis=1)
            acc[rows, :] = jnp.concatenate([alpha, alpha], axis=1) * acc[rows, :] + _dot(p, v)
            m_s[rows, :] = m_new

    def body(i, carry):
        step(2 * i, False)
        step(2 * i + 1, False)
        return carry

    lax.fori_loop(0, qi // 2, body, 0)

    @pl.when(qi % 2 == 1)
    def _():
        step(qi - 1, False)

    step(qi, True)

    lam = _lambda(lq1, lk1, lq2, lk2, lam_init)
    a = acc[...]
    o = a[:, :HEAD_W] / a[:, HEAD_W:]
    o = o[:t] - lam * o[t:]
    o = _rms(o, SUBLN_EPS) * sub_ref[...] * (1.0 - lam_init)
    o_ref[...] = o.astype(o_ref.dtype)


def _attn_prompt(q, k, v, lams, subln, batch, seq, lam_init, t):
    n, w = q.shape
    nq = seq // t
    lam_specs = [pl.BlockSpec((1, DIFF_HD), lambda b, h, i: (0, 0)) for _ in range(4)]
    return pl.pallas_call(
        functools.partial(_attn_prompt_kernel, t=t, lam_init=lam_init),
        grid=(batch, DIFF_HEADS, nq),
        in_specs=lam_specs + [
            pl.BlockSpec((t, HEAD_W), lambda b, h, i: (b * nq + i, h)),
            pl.BlockSpec((seq, HEAD_W), lambda b, h, i: (b, h)),
            pl.BlockSpec((seq, HEAD_W), lambda b, h, i: (b, h)),
            pl.BlockSpec((1, HEAD_W), lambda b, h, i: (0, 0))],
        out_specs=pl.BlockSpec((t, HEAD_W), lambda b, h, i: (b * nq + i, h)),
        out_shape=jax.ShapeDtypeStruct((n, w), BF16),
        scratch_shapes=[pltpu.VMEM((2 * t, HEAD_W), BF16),
                        pltpu.VMEM((seq, 2 * HEAD_W), BF16),
                        pltpu.VMEM((2 * t, HEAD_W), F32),
                        pltpu.VMEM((2 * t, 2 * HEAD_W), F32)],
        compiler_params=_cparams(("arbitrary", "arbitrary", "arbitrary")),
        name="diff_attn_prompt",
    )(*lams, q, k, v, subln.reshape(1, HEAD_W))


def _attn_decode_kernel(pt_ref, lq1, lk1, lq2, lk2, q_ref, kn_ref, vn_ref, sub_ref, k_hbm, v_hbm,
                        o_ref, kbuf, vbuf, sem, qm, bias, m_s, l_s, acc, *, pages_per_step, lam_init):
    g = pages_per_step
    p_idx = pl.program_id(1)
    n_steps = pl.num_programs(1)
    lin = pl.program_id(0) * n_steps + p_idx
    slot = lin % 2

    def page_copies(step, slot):
        out = []
        for r in range(g):
            page = pt_ref[step * g + r]
            out.append(pltpu.make_async_copy(k_hbm.at[page], kbuf.at[slot, r], sem.at[slot, 0]))
            out.append(pltpu.make_async_copy(v_hbm.at[page], vbuf.at[slot, r], sem.at[slot, 1]))
        return out

    @pl.when(lin == 0)
    def _():
        for c in page_copies(0, 0):
            c.start()

    @pl.when(lin + 1 < pl.num_programs(0) * n_steps)
    def _():
        for c in page_copies(lin + 1, 1 - slot):
            c.start()

    for c in page_copies(lin, slot):
        c.wait()

    @pl.when(p_idx == 0)
    def _():
        qm[...] = _stack_components(q_ref[...] * (DIFF_HD ** -0.5)).astype(BF16)
        row = lax.broadcasted_iota(jnp.int32, bias.shape, 0)
        col = lax.broadcasted_iota(jnp.int32, bias.shape, 1)
        bias[...] = jnp.where(row % DIFF_HEADS == col % DIFF_HEADS, 0.0, -jnp.inf)
        m_s[...] = jnp.full(m_s.shape, -jnp.inf, F32)
        l_s[...] = jnp.zeros(l_s.shape, F32)
        acc[...] = jnp.zeros(acc.shape, F32)

    scores = [lax.dot_general(qm[...], kbuf[slot, r].astype(BF16), _NT, preferred_element_type=F32)
              + bias[...] for r in range(g)]
    m_new = m_s[...]
    for s in scores:
        m_new = jnp.maximum(m_new, jnp.max(s, axis=-1, keepdims=True))
    alpha = jnp.exp(m_s[...] - m_new)
    l_new = alpha * l_s[...]
    a_new = alpha * acc[...]
    for r, s in enumerate(scores):
        p = jnp.exp(s - m_new)
        l_new = l_new + jnp.sum(p, axis=-1, keepdims=True)
        a_new = a_new + _dot(p.astype(BF16), vbuf[slot, r].astype(BF16))
    m_s[...] = m_new
    l_s[...] = l_new
    acc[...] = a_new

    @pl.when(p_idx == pl.num_programs(1) - 1)
    def _():
        kn = jnp.concatenate([kn_ref[...], kn_ref[...]], axis=0)
        vn = jnp.concatenate([vn_ref[...], vn_ref[...]], axis=0)
        s = jnp.sum(qm[...].astype(F32) * kn, axis=-1, keepdims=True)
        m_fin = jnp.maximum(m_new, s)
        beta = jnp.exp(m_new - m_fin)
        p = jnp.exp(s - m_fin)
        o = (beta * a_new + p * vn) / (beta * l_new + p)
        lam = _lambda(lq1, lk1, lq2, lk2, lam_init)
        o = o[:DIFF_HEADS] - lam * o[DIFF_HEADS:]
        o_ref[...] = _rms(o, SUBLN_EPS) * sub_ref[...] * (1.0 - lam_init)


def _attn_decode(q, k_new, v_new, cache_k, cache_v, page_base, page_table, lams, subln,
                 lam_init, pages_per_step):
    nb, d = q.shape
    n_pages = page_table.shape[1]
    g = pages_per_step
    steps = n_pages // g
    rows = PAGE_SIZE * DIFF_HEADS
    pt_flat = (page_table + page_base).reshape(-1).astype(jnp.int32)

    head_spec = pl.BlockSpec((None, DIFF_HEADS, HEAD_W), lambda b, p, pt: (b, 0, 0))
    lam_specs = [pl.BlockSpec((1, DIFF_HD), lambda b, p, pt: (0, 0)) for _ in range(4)]
    grid_spec = pltpu.PrefetchScalarGridSpec(
        num_scalar_prefetch=1,
        grid=(nb, steps),
        in_specs=lam_specs + [head_spec, head_spec, head_spec,
                              pl.BlockSpec((1, HEAD_W), lambda b, p, pt: (0, 0))]
        + [pl.BlockSpec(memory_space=pl.ANY), pl.BlockSpec(memory_space=pl.ANY)],
        out_specs=head_spec,
        scratch_shapes=[pltpu.VMEM((2, g, rows, HEAD_W), F32),
                        pltpu.VMEM((2, g, rows, HEAD_W), F32),
                        pltpu.SemaphoreType.DMA((2, 2)),
                        pltpu.VMEM((2 * DIFF_HEADS, HEAD_W), BF16),
                        pltpu.VMEM((2 * DIFF_HEADS, rows), F32),
                        pltpu.VMEM((2 * DIFF_HEADS, 1), F32),
                        pltpu.VMEM((2 * DIFF_HEADS, 1), F32),
                        pltpu.VMEM((2 * DIFF_HEADS, HEAD_W), F32)])
    per_head = lambda a: a.astype(F32).reshape(nb, DIFF_HEADS, HEAD_W)
    out = pl.pallas_call(
        functools.partial(_attn_decode_kernel, pages_per_step=g, lam_init=lam_init),
        grid_spec=grid_spec,
        out_shape=jax.ShapeDtypeStruct((nb, DIFF_HEADS, HEAD_W), F32),
        compiler_params=_cparams(("arbitrary", "arbitrary")),
        name="diff_attn_decode",
    )(pt_flat, *lams, per_head(q), per_head(k_new), per_head(v_new),
      subln.reshape(1, HEAD_W), cache_k, cache_v)
    return out.reshape(nb, d).astype(BF16)


def _rotate(x, cos, sin):
    half = x.shape[-1] // 2
    x1, x2 = x[:, :half], x[:, half:]
    return jnp.concatenate([x1 * cos - x2 * sin, x2 * cos + x1 * sin], axis=-1)


def _ret_prompt_kernel(lg_ref, q_ref, k_ref, v_ref, g_ref, cos_ref, sin_ref, o_ref, st_ref,
                       state, dmask, *, c):
    h = pl.program_id(1)
    ci = pl.program_id(2)
    lg = lg_ref[h]
    idx = lax.broadcasted_iota(jnp.int32, (c, 1), 0).astype(F32)

    @pl.when(ci == 0)
    def _():
        state[...] = jnp.zeros(state.shape, F32)
        row = lax.broadcasted_iota(jnp.int32, (c, c), 0)
        col = lax.broadcasted_iota(jnp.int32, (c, c), 1)
        diff = (row - col).astype(F32)
        dmask[...] = jnp.where(diff >= 0, jnp.exp(jnp.maximum(diff, 0.0) * lg), 0.0)

    cos, sin = cos_ref[...], sin_ref[...]
    q = _rotate(q_ref[...].astype(F32), cos, sin)
    k = _rotate(k_ref[...].astype(F32), cos, sin) * (q.shape[-1] ** -0.5)
    v = v_ref[...]
    q_dec = jnp.exp((idx + 1.0) * lg)
    k_dec = jnp.exp((c - 1.0 - idx) * lg)
    c_dec = jnp.exp(jnp.full((1, 1), c * lg, F32))

    s = lax.dot_general(q.astype(BF16), k.astype(BF16), _NT, preferred_element_type=F32) * dmask[...]
    st = state[...]
    o = _dot(s.astype(BF16), v) + _dot((q * q_dec).astype(BF16), st.astype(BF16))
    kd_t = (k * k_dec).T.astype(BF16)
    st = st * c_dec + _dot(kd_t, v)
    state[...] = st

    o = _rms(o, EPS)
    gate = g_ref[...].astype(F32)
    o_ref[...] = (o * (gate * _sigmoid(gate))).astype(o_ref.dtype)

    @pl.when(ci == pl.num_programs(2) - 1)
    def _():
        st_ref[...] = st


def _ret_prompt(z, col_q, col_k, col_v, col_g, batch, seq, cos, sin, lg, c):
    n = z.shape[0]
    dk = (col_k - col_q) // RET_HEADS
    dv = (col_g - col_v) // RET_HEADS
    nc = seq // c
    qo, ko, vo, go = col_q // dk, col_k // dk, col_v // dv, col_g // dv
    grid_spec = pltpu.PrefetchScalarGridSpec(
        num_scalar_prefetch=0,
        grid=(batch, RET_HEADS, nc),
        in_specs=[pl.BlockSpec(memory_space=pltpu.SMEM),
                  pl.BlockSpec((c, dk), lambda b, h, i: (b * nc + i, qo + h)),
                  pl.BlockSpec((c, dk), lambda b, h, i: (b * nc + i, ko + h)),
                  pl.BlockSpec((c, dv), lambda b, h, i: (b * nc + i, vo + h)),
                  pl.BlockSpec((c, dv), lambda b, h, i: (b * nc + i, go + h)),
                  pl.BlockSpec((c, dk // 2), lambda b, h, i: (i, 0)),
                  pl.BlockSpec((c, dk // 2), lambda b, h, i: (i, 0))],
        out_specs=[pl.BlockSpec((c, dv), lambda b, h, i: (b * nc + i, h)),
                   pl.BlockSpec((None, None, dk, dv), lambda b, h, i: (b, h, 0, 0))],
        scratch_shapes=[pltpu.VMEM((dk, dv), F32), pltpu.VMEM((c, c), F32)])
    return pl.pallas_call(
        functools.partial(_ret_prompt_kernel, c=c),
        grid_spec=grid_spec,
        out_shape=[jax.ShapeDtypeStruct((n, RET_HEADS * dv), BF16),
                   jax.ShapeDtypeStruct((batch, RET_HEADS, dk, dv), F32)],
        compiler_params=_cparams(("arbitrary", "arbitrary", "arbitrary")),
        name="retention_prompt",
    )(lg, z, z, z, z, cos, sin)


def _ret_decode_kernel(lg_ref, q_ref, k_ref, v_ref, g_ref, cos_ref, sin_ref, st_ref, o_ref, so_ref):
    dk, dv = st_ref.shape[1], st_ref.shape[2]
    cos, sin = cos_ref[...], sin_ref[...]
    for h in range(RET_HEADS):
        gamma = jnp.exp(jnp.full((1, 1), lg_ref[h], F32))
        q = _rotate(q_ref[:, h * dk:(h + 1) * dk].astype(F32), cos, sin)
        k = _rotate(k_ref[:, h * dk:(h + 1) * dk].astype(F32), cos, sin) * (dk ** -0.5)
        v = v_ref[:, h * dv:(h + 1) * dv].astype(F32)
        st = st_ref[h]
        q_col = jnp.broadcast_to(q * gamma, (LANES, dk)).T[:, 0:1]
        k_col = jnp.broadcast_to(k, (LANES, dk)).T[:, 0:1]
        qk = jnp.sum(q * k, axis=-1, keepdims=True)
        o = qk * v + jnp.sum(q_col * st, axis=0, keepdims=True)
        so_ref[h] = st * gamma + k_col * v
        o = _rms(o, EPS)
        gate = g_ref[:, h * dv:(h + 1) * dv].astype(F32)
        o_ref[:, h * dv:(h + 1) * dv] = (o * (gate * _sigmoid(gate))).astype(o_ref.dtype)


def _ret_decode(z, col_q, col_k, col_v, col_g, state, cos, sin, lg):
    nb = z.shape[0]
    _, _, dk, dv = state.shape
    z3 = z.reshape(nb, 1, z.shape[1])
    wq, wv = RET_HEADS * dk, RET_HEADS * dv

    def row(width, col):
        return pl.BlockSpec((None, 1, width), lambda b: (b, 0, col // width))

    o, st = pl.pallas_call(
        _ret_decode_kernel,
        grid=(nb,),
        in_specs=[pl.BlockSpec(memory_space=pltpu.SMEM),
                  row(wq, col_q), row(wq, col_k), row(wv, col_v), row(wv, col_g),
                  pl.BlockSpec((1, dk // 2), lambda b: (0, 0)),
                  pl.BlockSpec((1, dk // 2), lambda b: (0, 0)),
                  pl.BlockSpec((None, RET_HEADS, dk, dv), lambda b: (b, 0, 0, 0))],
        out_specs=[pl.BlockSpec((None, 1, wv), lambda b: (b, 0, 0)),
                   pl.BlockSpec((None, RET_HEADS, dk, dv), lambda b: (b, 0, 0, 0))],
        out_shape=[jax.ShapeDtypeStruct((nb, 1, wv), BF16),
                   jax.ShapeDtypeStruct(state.shape, F32)],
        compiler_params=_cparams(("arbitrary",)),
        name="retention_decode",
    )(lg, z3, z3, z3, z3, cos, sin, state)
    return o.reshape(nb, wv), st


def _postmix_kernel(oa_ref, ob_ref, ga_ref, gb_ref, x_ref, gt1_ref, sc2_ref, sh2_ref,
                    wa_ref, wb_ref, wo_ref, gpost_ref, gpre_ref, wr_ref, br_ref,
                    x1_ref, h2_ref, ti_ref, tg_ref):
    a = _dot(oa_ref[...], wa_ref[...])
    b = _dot(ob_ref[...], wb_ref[...])
    merged = _sigmoid(ga_ref[...].astype(F32)) * a + _sigmoid(gb_ref[...].astype(F32)) * b
    y = _dot(merged.astype(BF16), wo_ref[...])
    x1 = x_ref[...] + gt1_ref[...] * (_rms(y, EPS) * gpost_ref[...])
    x1_ref[...] = x1
    h2 = _rms(x1, EPS) * gpre_ref[...] * (1.0 + sc2_ref[...]) + sh2_ref[...]
    h2_ref[...] = h2

    logits = jnp.dot(h2, wr_ref[...], preferred_element_type=F32,
                     precision=lax.Precision.HIGHEST) + br_ref[...]
    lane = lax.broadcasted_iota(jnp.int32, logits.shape, 1)
    cur = logits
    vals, idxs = [], []
    for _ in range(TOP_K):
        m = jnp.max(cur, axis=-1, keepdims=True)
        idx = jnp.min(jnp.where(cur == m, lane, LANES), axis=-1, keepdims=True)
        vals.append(m)
        idxs.append(idx)
        cur = jnp.where(lane == idx, NEG * 2.0, cur)
    exps = [jnp.exp(v - vals[0]) for v in vals]
    denom = exps[0]
    for e in exps[1:]:
        denom = denom + e
    ti = jnp.zeros(logits.shape, jnp.int32)
    tg = jnp.zeros(logits.shape, F32)
    for kk in range(TOP_K):
        ti = jnp.where(lane == kk, idxs[kk], ti)
        tg = jnp.where(lane == kk, exps[kk] / denom, tg)
    ti_ref[...] = ti
    tg_ref[...] = tg


def _postmix(o_a, o_b, z, col_ga, col_gb, x, mod, wa, wb, wo, gpost, gpre, wr, br):
    n, d = x.shape
    tm = mod.tm
    da, db = o_a.shape[1], o_b.shape[1]
    const = lambda shape: pl.BlockSpec(shape, lambda i: (0, 0))
    rows = lambda w: pl.BlockSpec((tm, w), lambda i: (i, 0))
    return pl.pallas_call(
        _postmix_kernel,
        grid=(n // tm,),
        in_specs=[rows(da), rows(db),
                  pl.BlockSpec((tm, d), lambda i: (i, col_ga // d)),
                  pl.BlockSpec((tm, d), lambda i: (i, col_gb // d)),
                  rows(d), mod.spec(2), mod.spec(4), mod.spec(3),
                  const((da, d)), const((db, d)), const((d, d)),
                  const((1, d)), const((1, d)), const((d, LANES)), const((1, LANES))],
        out_specs=[rows(d), rows(d), rows(LANES), rows(LANES)],
        out_shape=[jax.ShapeDtypeStruct((n, d), F32), jax.ShapeDtypeStruct((n, d), F32),
                   jax.ShapeDtypeStruct((n, LANES), jnp.int32),
                   jax.ShapeDtypeStruct((n, LANES), F32)],
        compiler_params=_cparams(("arbitrary",)),
        name="postmix_router",
    )(o_a, o_b, z, z, x, mod.arr, mod.arr, mod.arr, wa, wb, wo,
      gpost.reshape(1, d), gpre.reshape(1, d), wr, br)


def _row_copy(src_hbm, row, dst, r, sem):
    return pltpu.make_async_copy(src_hbm.at[pl.ds(row, 1)], dst.at[pl.ds(r, 1)], sem)


def _gather_kernel(tok_ref, h_hbm, o_ref, sem, *, tm):
    base = pl.program_id(0) * tm

    def issue(r, carry):
        _row_copy(h_hbm, tok_ref[base + r], o_ref, r, sem).start()
        return carry

    lax.fori_loop(0, tm, issue, 0, unroll=16)

    def wait(r, carry):
        _row_copy(h_hbm, 0, o_ref, r, sem).wait()
        return carry

    lax.fori_loop(0, tm, wait, 0, unroll=16)


def _gather_rows(h, slot_tok, tm):
    p = slot_tok.shape[0]
    d = h.shape[1]
    grid_spec = pltpu.PrefetchScalarGridSpec(
        num_scalar_prefetch=1,
        grid=(p // tm,),
        in_specs=[pl.BlockSpec(memory_space=pl.ANY)],
        out_specs=pl.BlockSpec((tm, d), lambda t, tok: (t, 0)),
        scratch_shapes=[pltpu.SemaphoreType.DMA(())])
    return pl.pallas_call(
        functools.partial(_gather_kernel, tm=tm),
        grid_spec=grid_spec,
        out_shape=jax.ShapeDtypeStruct((p, d), h.dtype),
        compiler_params=_cparams(("arbitrary",)),
        name="moe_gather",
    )(slot_tok, h)


def _ffn_kernel(te_ref, nv_ref, x_ref, wg_ref, wl_ref, bg_ref, bl_ref, wd_ref, bd_ref, o_ref):
    del te_ref
    t = pl.program_id(0)

    @pl.when(t < nv_ref[0])
    def _():
        x = x_ref[...].astype(BF16)
        g = jnp.minimum(_dot(x, wg_ref[...]) + bg_ref[...], SWIGLU_LIMIT)
        l = jnp.clip(_dot(x, wl_ref[...]) + bl_ref[...], -SWIGLU_LIMIT, SWIGLU_LIMIT)
        act = g * _sigmoid(SWIGLU_ALPHA * g) * (l + 1.0)
        o_ref[...] = _dot(act.astype(BF16), wd_ref[...]) + bd_ref[...]

    @pl.when(t >= nv_ref[0])
    def _():
        o_ref[...] = jnp.zeros(o_ref.shape, o_ref.dtype)


def _expert_ffn(xg, tile_expert, n_valid, w_up, b_up, w_dn, b_dn, tm):
    p, d = xg.shape
    dff = w_up.shape[-1]
    e = w_up.shape[0]
    grid_spec = pltpu.PrefetchScalarGridSpec(
        num_scalar_prefetch=2,
        grid=(p // tm,),
        in_specs=[pl.BlockSpec((tm, d), lambda t, te, nv: (t, 0)),
                  pl.BlockSpec((None, None, d, dff), lambda t, te, nv: (te[t], 0, 0, 0)),
                  pl.BlockSpec((None, None, d, dff), lambda t, te, nv: (te[t], 1, 0, 0)),
                  pl.BlockSpec((None, None, 1, dff), lambda t, te, nv: (te[t], 0, 0, 0)),
                  pl.BlockSpec((None, None, 1, dff), lambda t, te, nv: (te[t], 1, 0, 0)),
                  pl.BlockSpec((None, dff, d), lambda t, te, nv: (te[t], 0, 0)),
                  pl.BlockSpec((None, 1, d), lambda t, te, nv: (te[t], 0, 0))],
        out_specs=pl.BlockSpec((tm, d), lambda t, te, nv: (t, 0)))
    return pl.pallas_call(
        _ffn_kernel,
        grid_spec=grid_spec,
        out_shape=jax.ShapeDtypeStruct((p, d), F32),
        compiler_params=_cparams(("arbitrary",)),
        name="moe_expert_ffn",
    )(tile_expert, n_valid, xg, w_up, w_up, b_up.reshape(e, 2, 1, dff), b_up.reshape(e, 2, 1, dff),
      w_dn, b_dn.reshape(e, 1, d))


def _combine_kernel(pos_ref, y_hbm, tg_ref, x1_ref, gt2_ref, gpost_ref, o_ref, buf, sem, *, tm):
    base = pl.program_id(0) * tm * TOP_K

    def issue(r, carry):
        for kk in range(TOP_K):
            _row_copy(y_hbm, pos_ref[base + r * TOP_K + kk], buf.at[kk], r, sem).start()
        return carry

    lax.fori_loop(0, tm, issue, 0, unroll=4)

    def wait(r, carry):
        for kk in range(TOP_K):
            _row_copy(y_hbm, 0, buf.at[kk], r, sem).wait()
        return carry

    lax.fori_loop(0, tm, wait, 0, unroll=4)

    tg = tg_ref[...]
    f = tg[:, 0:1] * buf[0]
    for kk in range(1, TOP_K):
        f = f + tg[:, kk:kk + 1] * buf[kk]
    o_ref[...] = x1_ref[...] + gt2_ref[...] * (_rms(f, EPS) * gpost_ref[...])


def _combine(ys, pos, tg, x1, mod, gpost):
    n, d = x1.shape
    tm = mod.tm
    grid_spec = pltpu.PrefetchScalarGridSpec(
        num_scalar_prefetch=1,
        grid=(n // tm,),
        in_specs=[pl.BlockSpec(memory_space=pl.ANY),
                  pl.BlockSpec((tm, LANES), lambda i, pos: (i, 0)),
                  pl.BlockSpec((tm, d), lambda i, pos: (i, 0)),
                  _with_prefetch(mod.spec(5)),
                  pl.BlockSpec((1, d), lambda i, pos: (0, 0))],
        out_specs=pl.BlockSpec((tm, d), lambda i, pos: (i, 0)),
        scratch_shapes=[pltpu.VMEM((TOP_K, tm, d), F32), pltpu.SemaphoreType.DMA(())])
    return pl.pallas_call(
        functools.partial(_combine_kernel, tm=tm),
        grid_spec=grid_spec,
        out_shape=jax.ShapeDtypeStruct((n, d), F32),
        compiler_params=_cparams(("arbitrary",)),
        name="moe_combine",
    )(pos, ys, tg, x1, mod.arr, gpost.reshape(1, d))


def _with_prefetch(spec):
    inner = spec.index_map
    return pl.BlockSpec(spec.block_shape, lambda i, _s: inner(i))


def _route(top_i, n_experts, tm):
    n = top_i.shape[0]
    nk = n * TOP_K
    flat_e = top_i.reshape(-1)
    onehot = (flat_e[:, None] == jnp.arange(n_experts, dtype=jnp.int32)[None, :]).astype(jnp.int32)
    csum = jnp.cumsum(onehot, axis=0)
    counts = csum[-1]
    rank = jnp.sum(onehot * csum, axis=1) - 1
    pcounts = (counts + tm - 1) // tm * tm
    pend = jnp.cumsum(pcounts)
    pstart = pend - pcounts
    pos = jnp.sum(onehot * pstart[None, :], axis=1) + rank
    n_slots = (nk + n_experts * (tm - 1) + tm - 1) // tm * tm
    slot_tok = jnp.zeros((n_slots,), jnp.int32).at[pos].set(jnp.arange(nk, dtype=jnp.int32) // TOP_K)
    tiles = jnp.arange(n_slots // tm, dtype=jnp.int32) * tm
    tile_expert = jnp.minimum(jnp.searchsorted(pend, tiles, side='right'), n_experts - 1).astype(jnp.int32)
    n_valid = (pend[-1] // tm).astype(jnp.int32).reshape(1)
    return slot_tok, pos.astype(jnp.int32), tile_expert, n_valid


def _moe(h2, top_i, tg, x1, mod, gpost, w_up, b_up, w_dn, b_dn, tm_ffn):
    n_experts = w_up.shape[0]
    slot_tok, pos, tile_expert, n_valid = _route(top_i, n_experts, tm_ffn)
    xg = _gather_rows(h2, slot_tok, _pick(slot_tok.shape[0], 4 * tm_ffn))
    ys = _expert_ffn(xg, tile_expert, n_valid, w_up, b_up, w_dn, b_dn, tm_ffn)
    return _combine(ys, pos, tg, x1, mod, gpost)


def _layer(x, mod, lw, attend, retain, d):
    tm = mod.tm
    cq, ck, cv = 0, d, 2 * d
    rq, rk, rv, rg = 3 * d, 4 * d, 5 * d, 7 * d
    ga, gb = 9 * d, 10 * d
    h = _prenorm(x, lw['g_pre1'], mod, 1, 0)
    tm_mm = _pick(x.shape[0], 512)
    dq, = _matmul(h, lw['w_in'], cq, d, (BF16,), tm_mm)
    k32, k16 = _matmul(h, lw['w_in'], ck, d, (F32, BF16), tm_mm)
    v32, v16 = _matmul(h, lw['w_in'], cv, d, (F32, BF16), tm_mm)
    z, = _matmul(h, lw['w_in'], rq, 8 * d, (BF16,), tm_mm)
    o_a = attend(dq, k16, v16)
    o_b, state = retain(z, rq - rq, rk - rq, rv - rq, rg - rq)
    x1, h2, top_i, tg = _postmix(o_a, o_b, z, ga - rq, gb - rq, x, mod,
                                 lw['w_a'], lw['w_b'], lw['w_o'], lw['g_post1'], lw['g_pre2'],
                                 lw['w_r'], lw['b_r'])
    y = _moe(h2, top_i[:, :TOP_K], tg, x1, mod, lw['g_post2'],
             lw['w_up'], lw['b_up'], lw['w_dn'], lw['b_dn'], lw['tm_ffn'])
    return y, k32, v32, state


def _pick(n, pref):
    t = min(n, pref)
    while n % t:
        t //= 2
    return t


def kernel(x_prompt, x_sample, cache_k, cache_v, state_ret, page_table, c_prompt, c_sample, w_ada, b_ada, norm_pre_mix, norm_post_mix, norm_pre_ffn, norm_post_ffn, w_in, lambda_q1, lambda_k1, lambda_q2, lambda_k2, diff_subln, w_branch_a, w_branch_b, w_out, w_router, b_router, w_exp_up, b_exp_up, w_exp_down, b_exp_down):
    batch, seq, d = x_prompt.shape
    nb, dec_seq, _ = x_sample.shape
    assert dec_seq == 1
    depth = w_ada.shape[0]
    n_phys = cache_k.shape[1]
    n_pages = page_table.shape[1]
    past = n_pages * PAGE_SIZE
    n_experts = w_router.shape[-1]
    dff = w_exp_down.shape[-2]
    dk = d // RET_HEADS

    inv = ROPE_BASE ** (-jnp.arange(dk // 2, dtype=F32) * 2.0 / dk)
    ang_p = jnp.arange(seq, dtype=jnp.int32).astype(F32)[:, None] * inv[None, :]
    ang_s = jnp.full((1, 1), past, jnp.int32).astype(F32) * inv[None, :]
    cos_p, sin_p, cos_s, sin_s = jnp.cos(ang_p), jnp.sin(ang_p), jnp.cos(ang_s), jnp.sin(ang_s)
    lg = jnp.log(1.0 - 2.0 ** (-5.0 - jnp.arange(RET_HEADS, dtype=F32)))

    cache_k2 = cache_k.reshape(depth * n_phys, PAGE_SIZE * DIFF_HEADS, HEAD_W)
    cache_v2 = cache_v.reshape(depth * n_phys, PAGE_SIZE * DIFF_HEADS, HEAD_W)

    tm_p = _pick(batch * seq, 256)
    t_attn = _pick(seq, 512)
    c_ret = _pick(seq, 256)
    pages_per_step = _pick(n_pages, 8)

    yp = x_prompt.reshape(batch * seq, d)
    ys = x_sample.reshape(nb, d)
    c_all = jnp.concatenate([c_prompt, c_sample], axis=0)
    pad = (-c_all.shape[0]) % 8
    c_all = jnp.pad(c_all, ((0, pad), (0, 0)))
    outs = [[] for _ in range(6)]
    for l in range(depth):
        lam_init = 0.8 - 0.6 * math.exp(-0.3 * l)
        lams = tuple(a[l].reshape(1, DIFF_HD) for a in (lambda_q1, lambda_k1, lambda_q2, lambda_k2))
        table = _ada(c_all, w_ada[l], b_ada[l])
        mod_p = _Mod(table[:batch], d, tm_p, seq)
        mod_s = _Mod(table[batch:batch + nb], d, nb, 1)
        w_up = w_exp_up[l].reshape(n_experts, d, dff, 2).transpose(0, 3, 1, 2).astype(BF16)
        b_up = b_exp_up[l].reshape(n_experts, dff, 2).transpose(0, 2, 1)
        w_r = jnp.pad(w_router[l], ((0, 0), (0, LANES - n_experts)))
        b_r = jnp.pad(b_router[l], (0, LANES - n_experts), constant_values=NEG).reshape(1, LANES)
        lw = dict(g_pre1=norm_pre_mix[l], g_post1=norm_post_mix[l], g_pre2=norm_pre_ffn[l],
                  g_post2=norm_post_ffn[l], w_in=w_in[l],
                  w_a=w_branch_a[l].astype(BF16), w_b=w_branch_b[l].astype(BF16),
                  w_o=w_out[l].astype(BF16), w_r=w_r, b_r=b_r,
                  w_up=w_up, b_up=b_up, w_dn=w_exp_down[l].astype(BF16), b_dn=b_exp_down[l])

        attend_p = lambda q, k, v: _attn_prompt(q, k, v, lams, diff_subln[l], batch, seq, lam_init, t_attn)
        retain_p = lambda z, a, b, c, g: _ret_prompt(z, a, b, c, g, batch, seq, cos_p, sin_p, lg, c_ret)
        yp, kp, vp, rp = _layer(yp, mod_p, dict(lw, tm_ffn=_pick(batch * seq, 256)), attend_p, retain_p, d)

        attend_s = lambda q, k, v: _attn_decode(q, k, v, cache_k2, cache_v2, l * n_phys, page_table,
                                                lams, diff_subln[l], lam_init, pages_per_step)
        retain_s = lambda z, a, b, c, g: _ret_decode(z, a, b, c, g, state_ret[l], cos_s, sin_s, lg)
        ys, ks, vs, rs = _layer(ys, mod_s, dict(lw, tm_ffn=16), attend_s, retain_s, d)

        for lst, val in zip(outs, (kp.reshape(batch, seq, DIFF_HEADS, HEAD_W),
                                   vp.reshape(batch, seq, DIFF_HEADS, HEAD_W),
                                   ks.reshape(nb, 1, DIFF_HEADS, HEAD_W),
                                   vs.reshape(nb, 1, DIFF_HEADS, HEAD_W), rp, rs)):
            lst.append(val)
    return (yp.reshape(batch, seq, d), ys.reshape(nb, 1, d)) + tuple(jnp.stack(o) for o in outs)
```

```python
import functools
import math

import jax
import jax.numpy as jnp
from jax import lax
from jax.experimental import pallas as pl
from jax.experimental.pallas import tpu as pltpu

F32 = jnp.float32
BF16 = jnp.bfloat16

DIFF_HEADS = 8
DIFF_HD = 64
HEAD_W = 2 * DIFF_HD
RET_HEADS = 4
PAGE_SIZE = 128
TOP_K = 4
ROPE_BASE = 10000.0
SWIGLU_LIMIT = 7.0
SWIGLU_ALPHA = 1.702
EPS = 1e-6
SUBLN_EPS = 1e-5
LANES = 128
VMEM_LIMIT = 52 * 1024 * 1024
NEG = -1e30
ATTN_ROW_GROUP = 128
ATTN_UNROLL = 4

_NT = (((1,), (1,)), ((), ()))


def _cparams(sem):
    return pltpu.CompilerParams(dimension_semantics=sem, vmem_limit_bytes=VMEM_LIMIT)


def _sigmoid(x):
    return 1.0 / (1.0 + jnp.exp(-x))


def _rms(x, eps):
    return x * lax.rsqrt(jnp.mean(x * x, axis=-1, keepdims=True) + eps)


def _dot(a, b):
    return jnp.dot(a, b, preferred_element_type=F32)


def _ada_kernel(c_ref, w_ref, b_ref, o_ref):
    c = c_ref[...]
    s = (c * _sigmoid(c)).astype(BF16)
    o_ref[...] = _dot(s, w_ref[...].astype(BF16)) + b_ref[...]


def _ada(c, w, b, tn=1024):
    r, d = c.shape
    n = w.shape[1]
    return pl.pallas_call(
        _ada_kernel,
        grid=(n // tn,),
        in_specs=[pl.BlockSpec((r, d), lambda j: (0, 0)),
                  pl.BlockSpec((d, tn), lambda j: (0, j)),
                  pl.BlockSpec((1, tn), lambda j: (0, j))],
        out_specs=pl.BlockSpec((r, tn), lambda j: (0, j)),
        out_shape=jax.ShapeDtypeStruct((r, n), F32),
        compiler_params=_cparams(("arbitrary",)),
        name="ada_mod",
    )(c, w, b.reshape(1, n))


class _Mod:
    def __init__(self, table, d, tm, rows_per_entry):
        self.d = d
        self.tm = tm
        if rows_per_entry == 1:
            self.arr = table
            self.per_row = True
        else:
            self.arr = table.reshape(table.shape[0], 1, table.shape[1])
            self.per_row = False
            self.tiles_per_entry = rows_per_entry // tm

    def spec(self, c):
        if self.per_row:
            return pl.BlockSpec((self.tm, self.d), lambda i: (i, c))
        tpe = self.tiles_per_entry
        return pl.BlockSpec((None, 1, self.d), lambda i: (i // tpe, 0, c))


def _prenorm_kernel(x_ref, g_ref, sc_ref, sh_ref, o_ref):
    y = _rms(x_ref[...], EPS) * g_ref[...]
    o_ref[...] = (y * (1.0 + sc_ref[...]) + sh_ref[...]).astype(o_ref.dtype)


def _prenorm(x, g, mod, c_scale, c_shift):
    n, d = x.shape
    tm = mod.tm
    return pl.pallas_call(
        _prenorm_kernel,
        grid=(n // tm,),
        in_specs=[pl.BlockSpec((tm, d), lambda i: (i, 0)),
                  pl.BlockSpec((1, d), lambda i: (0, 0)),
                  mod.spec(c_scale), mod.spec(c_shift)],
        out_specs=pl.BlockSpec((tm, d), lambda i: (i, 0)),
        out_shape=jax.ShapeDtypeStruct((n, d), BF16),
        compiler_params=_cparams(("arbitrary",)),
        name="prenorm",
    )(x, g.reshape(1, d), mod.arr, mod.arr)


def _mm_kernel(h_ref, w_ref, *rest):
    outs, wb = rest[:-1], rest[-1]

    @pl.when(pl.program_id(1) == 0)
    def _():
        wb[...] = w_ref[...].astype(BF16)

    acc = _dot(h_ref[...], wb[...])
    for o in outs:
        o[...] = acc.astype(o.dtype)


def _matmul(h, w, col0, ncols, out_dtypes, tm, tn=1024):
    n, k = h.shape
    off = col0 // tn
    outs = pl.pallas_call(
        _mm_kernel,
        grid=(ncols // tn, n // tm),
        in_specs=[pl.BlockSpec((tm, k), lambda j, i: (i, 0)),
                  pl.BlockSpec((k, tn), lambda j, i: (0, j + off))],
        out_specs=[pl.BlockSpec((tm, tn), lambda j, i: (i, j)) for _ in out_dtypes],
        out_shape=[jax.ShapeDtypeStruct((n, ncols), dt) for dt in out_dtypes],
        scratch_shapes=[pltpu.VMEM((k, tn), BF16)],
        compiler_params=_cparams(("arbitrary", "arbitrary")),
        name="in_proj",
    )(h, w)
    return outs


def _lambda(lq1, lk1, lq2, lk2, lam_init):
    a = jnp.sum(lq1[...] * lk1[...], axis=-1, keepdims=True)
    b = jnp.sum(lq2[...] * lk2[...], axis=-1, keepdims=True)
    return jnp.exp(a) - jnp.exp(b) + lam_init


def _stack_components(q):
    lane = lax.broadcasted_iota(jnp.int32, q.shape, 1)
    zero = jnp.zeros_like(q)
    return jnp.concatenate([jnp.where(lane < DIFF_HD, q, zero),
                            jnp.where(lane >= DIFF_HD, q, zero)], axis=0)


def _attn_prompt_kernel(lq1, lk1, lq2, lk2, q_ref, k_ref, v_ref, sub_ref, o_ref,
                        qs, vext, m_s, acc, *, t, lam_init):
    qi = pl.program_id(2)

    @pl.when(qi == 0)
    def _():
        vext[:, :HEAD_W] = v_ref[...]
        vext[:, HEAD_W:] = jnp.ones((vext.shape[0], HEAD_W), vext.dtype)

    qs[...] = _stack_components(q_ref[...].astype(F32) * (DIFF_HD ** -0.5)).astype(BF16)
    m_s[...] = jnp.full(m_s.shape, -jnp.inf, F32)
    acc[...] = jnp.zeros(acc.shape, F32)

    def step(ki, masked):
        start = pl.multiple_of(ki * t, t)
        k = k_ref[pl.ds(start, t), :]
        v = vext[pl.ds(start, t), :]
        group = min(ATTN_ROW_GROUP, t)
        for g0 in range(0, 2 * t, group):
            rows = pl.ds(g0, group)
            s = lax.dot_general(qs[rows, :], k, _NT, preferred_element_type=F32)
            if masked:
                row = lax.broadcasted_iota(jnp.int32, s.shape, 0) + (g0 % t)
                col = lax.broadcasted_iota(jnp.int32, s.shape, 1)
                s = jnp.where(col <= row, s, -jnp.inf)
            chunks = [s[:, j * LANES:(j + 1) * LANES] for j in range(t // LANES)]
            cmax = chunks[0]
            for c in chunks[1:]:
                cmax = jnp.maximum(cmax, c)
            m_old = m_s[rows, :]
            m_new = jnp.maximum(m_old, jnp.broadcast_to(jnp.max(cmax, axis=-1, keepdims=True), m_old.shape))
            alpha = jnp.exp(m_old - m_new)
            p = jnp.concatenate([jnp.exp(c - m_new).astype(BF16) for c in chunks], axis=1)
            acc[rows, :] = jnp.concatenate([alpha, alpha], axis=1) * acc[rows, :] + _dot(p, v)
            m_s[rows, :] = m_new

    def body(i, carry):
        for u in range(ATTN_UNROLL):
            step(ATTN_UNROLL * i + u, False)
        return carry

    lax.fori_loop(0, qi // ATTN_UNROLL, body, 0)

    rem = qi % ATTN_UNROLL
    bit = ATTN_UNROLL // 2
    while bit >= 1:
        def run(bit=bit):
            first = qi - rem % (2 * bit)
            for j in range(bit):
                step(first + j, False)
        pl.when((rem & bit) != 0)(run)
        bit //= 2

    step(qi, True)

    lam = _lambda(lq1, lk1, lq2, lk2, lam_init)
    a = acc[...]
    o = a[:, :HEAD_W] / a[:, HEAD_W:]
    o = o[:t] - lam * o[t:]
    o = _rms(o, SUBLN_EPS) * sub_ref[...] * (1.0 - lam_init)
    o_ref[...] = o.astype(o_ref.dtype)


def _attn_prompt(q, k, v, lams, subln, batch, seq, lam_init, t):
    n, w = q.shape
    nq = seq // t
    lam_specs = [pl.BlockSpec((1, DIFF_HD), lambda b, h, i: (0, 0)) for _ in range(4)]
    return pl.pallas_call(
        functools.partial(_attn_prompt_kernel, t=t, lam_init=lam_init),
        grid=(batch, DIFF_HEADS, nq),
        in_specs=lam_specs + [
            pl.BlockSpec((t, HEAD_W), lambda b, h, i: (b * nq + i, h)),
            pl.BlockSpec((seq, HEAD_W), lambda b, h, i: (b, h)),
            pl.BlockSpec((seq, HEAD_W), lambda b, h, i: (b, h)),
            pl.BlockSpec((1, HEAD_W), lambda b, h, i: (0, 0))],
        out_specs=pl.BlockSpec((t, HEAD_W), lambda b, h, i: (b * nq + i, h)),
        out_shape=jax.ShapeDtypeStruct((n, w), BF16),
        scratch_shapes=[pltpu.VMEM((2 * t, HEAD_W), BF16),
                        pltpu.VMEM((seq, 2 * HEAD_W), BF16),
                        pltpu.VMEM((2 * t, HEAD_W), F32),
                        pltpu.VMEM((2 * t, 2 * HEAD_W), F32)],
        compiler_params=_cparams(("arbitrary", "arbitrary", "arbitrary")),
        name="diff_attn_prompt",
    )(*lams, q, k, v, subln.reshape(1, HEAD_W))


def _attn_decode_kernel(pt_ref, lq1, lk1, lq2, lk2, q_ref, kn_ref, vn_ref, sub_ref, k_hbm, v_hbm,
                        o_ref, kbuf, vbuf, sem, qm, bias, m_s, l_s, acc, *, pages_per_step, lam_init):
    g = pages_per_step
    p_idx = pl.program_id(1)
    n_steps = pl.num_programs(1)
    lin = pl.program_id(0) * n_steps + p_idx
    slot = lin % 2

    def page_copies(step, slot):
        out = []
        for r in range(g):
            page = pt_ref[step * g + r]
            out.append(pltpu.make_async_copy(k_hbm.at[page], kbuf.at[slot, r], sem.at[slot, 0]))
            out.append(pltpu.make_async_copy(v_hbm.at[page], vbuf.at[slot, r], sem.at[slot, 1]))
        return out

    @pl.when(lin == 0)
    def _():
        for c in page_copies(0, 0):
            c.start()

    @pl.when(lin + 1 < pl.num_programs(0) * n_steps)
    def _():
        for c in page_copies(lin + 1, 1 - slot):
            c.start()

    for c in page_copies(lin, slot):
        c.wait()

    @pl.when(p_idx == 0)
    def _():
        qm[...] = _stack_components(q_ref[...] * (DIFF_HD ** -0.5)).astype(BF16)
        row = lax.broadcasted_iota(jnp.int32, bias.shape, 0)
        col = lax.broadcasted_iota(jnp.int32, bias.shape, 1)
        bias[...] = jnp.where(row % DIFF_HEADS == col % DIFF_HEADS, 0.0, -jnp.inf)
        m_s[...] = jnp.full(m_s.shape, -jnp.inf, F32)
        l_s[...] = jnp.zeros(l_s.shape, F32)
        acc[...] = jnp.zeros(acc.shape, F32)

    scores = [lax.dot_general(qm[...], kbuf[slot, r].astype(BF16), _NT, preferred_element_type=F32)
              + bias[...] for r in range(g)]
    m_new = m_s[...]
    for s in scores:
        m_new = jnp.maximum(m_new, jnp.max(s, axis=-1, keepdims=True))
    alpha = jnp.exp(m_s[...] - m_new)
    l_new = alpha * l_s[...]
    a_new = alpha * acc[...]
    for r, s in enumerate(scores):
        p = jnp.exp(s - m_new)
        l_new = l_new + jnp.sum(p, axis=-1, keepdims=True)
        a_new = a_new + _dot(p.astype(BF16), vbuf[slot, r].astype(BF16))
    m_s[...] = m_new
    l_s[...] = l_new
    acc[...] = a_new

    @pl.when(p_idx == pl.num_programs(1) - 1)
    def _():
        kn = jnp.concatenate([kn_ref[...], kn_ref[...]], axis=0)
        vn = jnp.concatenate([vn_ref[...], vn_ref[...]], axis=0)
        s = jnp.sum(qm[...].astype(F32) * kn, axis=-1, keepdims=True)
        m_fin = jnp.maximum(m_new, s)
        beta = jnp.exp(m_new - m_fin)
        p = jnp.exp(s - m_fin)
        o = (beta * a_new + p * vn) / (beta * l_new + p)
        lam = _lambda(lq1, lk1, lq2, lk2, lam_init)
        o = o[:DIFF_HEADS] - lam * o[DIFF_HEADS:]
        o_ref[...] = _rms(o, SUBLN_EPS) * sub_ref[...] * (1.0 - lam_init)


def _attn_decode(q, k_new, v_new, cache_k, cache_v, page_base, page_table, lams, subln,
                 lam_init, pages_per_step):
    nb, d = q.shape
    n_pages = page_table.shape[1]
    g = pages_per_step
    steps = n_pages // g
    rows = PAGE_SIZE * DIFF_HEADS
    pt_flat = (page_table + page_base).reshape(-1).astype(jnp.int32)

    head_spec = pl.BlockSpec((None, DIFF_HEADS, HEAD_W), lambda b, p, pt: (b, 0, 0))
    lam_specs = [pl.BlockSpec((1, DIFF_HD), lambda b, p, pt: (0, 0)) for _ in range(4)]
    grid_spec = pltpu.PrefetchScalarGridSpec(
        num_scalar_prefetch=1,
        grid=(nb, steps),
        in_specs=lam_specs + [head_spec, head_spec, head_spec,
                              pl.BlockSpec((1, HEAD_W), lambda b, p, pt: (0, 0))]
        + [pl.BlockSpec(memory_space=pl.ANY), pl.BlockSpec(memory_space=pl.ANY)],
        out_specs=head_spec,
        scratch_shapes=[pltpu.VMEM((2, g, rows, HEAD_W), F32),
                        pltpu.VMEM((2, g, rows, HEAD_W), F32),
                        pltpu.SemaphoreType.DMA((2, 2)),
                        pltpu.VMEM((2 * DIFF_HEADS, HEAD_W), BF16),
                        pltpu.VMEM((2 * DIFF_HEADS, rows), F32),
                        pltpu.VMEM((2 * DIFF_HEADS, 1), F32),
                        pltpu.VMEM((2 * DIFF_HEADS, 1), F32),
                        pltpu.VMEM((2 * DIFF_HEADS, HEAD_W), F32)])
    per_head = lambda a: a.astype(F32).reshape(nb, DIFF_HEADS, HEAD_W)
    out = pl.pallas_call(
        functools.partial(_attn_decode_kernel, pages_per_step=g, lam_init=lam_init),
        grid_spec=grid_spec,
        out_shape=jax.ShapeDtypeStruct((nb, DIFF_HEADS, HEAD_W), F32),
        compiler_params=_cparams(("arbitrary", "arbitrary")),
        name="diff_attn_decode",
    )(pt_flat, *lams, per_head(q), per_head(k_new), per_head(v_new),
      subln.reshape(1, HEAD_W), cache_k, cache_v)
    return out.reshape(nb, d).astype(BF16)


def _rotate(x, cos, sin):
    half = x.shape[-1] // 2
    x1, x2 = x[:, :half], x[:, half:]
    return jnp.concatenate([x1 * cos - x2 * sin, x2 * cos + x1 * sin], axis=-1)


def _ret_prompt_kernel(lg_ref, q_ref, k_ref, v_ref, g_ref, cos_ref, sin_ref, o_ref, st_ref,
                       state, dmask, *, c):
    h = pl.program_id(1)
    ci = pl.program_id(2)
    lg = lg_ref[h]
    idx = lax.broadcasted_iota(jnp.int32, (c, 1), 0).astype(F32)

    @pl.when(ci == 0)
    def _():
        state[...] = jnp.zeros(state.shape, F32)
        row = lax.broadcasted_iota(jnp.int32, (c, c), 0)
        col = lax.broadcasted_iota(jnp.int32, (c, c), 1)
        diff = (row - col).astype(F32)
        dmask[...] = jnp.where(diff >= 0, jnp.exp(jnp.maximum(diff, 0.0) * lg), 0.0)

    cos, sin = cos_ref[...], sin_ref[...]
    q = _rotate(q_ref[...].astype(F32), cos, sin)
    k = _rotate(k_ref[...].astype(F32), cos, sin) * (q.shape[-1] ** -0.5)
    v = v_ref[...]
    q_dec = jnp.exp((idx + 1.0) * lg)
    k_dec = jnp.exp((c - 1.0 - idx) * lg)
    c_dec = jnp.exp(jnp.full((1, 1), c * lg, F32))

    s = lax.dot_general(q.astype(BF16), k.astype(BF16), _NT, preferred_element_type=F32) * dmask[...]
    st = state[...]
    o = _dot(s.astype(BF16), v) + _dot((q * q_dec).astype(BF16), st.astype(BF16))
    kd_t = (k * k_dec).T.astype(BF16)
    st = st * c_dec + _dot(kd_t, v)
    state[...] = st

    o = _rms(o, EPS)
    gate = g_ref[...].astype(F32)
    o_ref[...] = (o * (gate * _sigmoid(gate))).astype(o_ref.dtype)

    @pl.when(ci == pl.num_programs(2) - 1)
    def _():
        st_ref[...] = st


def _ret_prompt(z, col_q, col_k, col_v, col_g, batch, seq, cos, sin, lg, c):
    n = z.shape[0]
    dk = (col_k - col_q) // RET_HEADS
    dv = (col_g - col_v) // RET_HEADS
    nc = seq // c
    qo, ko, vo, go = col_q // dk, col_k // dk, col_v // dv, col_g // dv
    grid_spec = pltpu.PrefetchScalarGridSpec(
        num_scalar_prefetch=0,
        grid=(batch, RET_HEADS, nc),
        in_specs=[pl.BlockSpec(memory_space=pltpu.SMEM),
                  pl.BlockSpec((c, dk), lambda b, h, i: (b * nc + i, qo + h)),
                  pl.BlockSpec((c, dk), lambda b, h, i: (b * nc + i, ko + h)),
                  pl.BlockSpec((c, dv), lambda b, h, i: (b * nc + i, vo + h)),
                  pl.BlockSpec((c, dv), lambda b, h, i: (b * nc + i, go + h)),
                  pl.BlockSpec((c, dk // 2), lambda b, h, i: (i, 0)),
                  pl.BlockSpec((c, dk // 2), lambda b, h, i: (i, 0))],
        out_specs=[pl.BlockSpec((c, dv), lambda b, h, i: (b * nc + i, h)),
                   pl.BlockSpec((None, None, dk, dv), lambda b, h, i: (b, h, 0, 0))],
        scratch_shapes=[pltpu.VMEM((dk, dv), F32), pltpu.VMEM((c, c), F32)])
    return pl.pallas_call(
        functools.partial(_ret_prompt_kernel, c=c),
        grid_spec=grid_spec,
        out_shape=[jax.ShapeDtypeStruct((n, RET_HEADS * dv), BF16),
                   jax.ShapeDtypeStruct((batch, RET_HEADS, dk, dv), F32)],
        compiler_params=_cparams(("arbitrary", "arbitrary", "arbitrary")),
        name="retention_prompt",
    )(lg, z, z, z, z, cos, sin)


def _ret_decode_kernel(lg_ref, q_ref, k_ref, v_ref, g_ref, cos_ref, sin_ref, st_ref, o_ref, so_ref):
    dk, dv = st_ref.shape[1], st_ref.shape[2]
    cos, sin = cos_ref[...], sin_ref[...]
    for h in range(RET_HEADS):
        gamma = jnp.exp(jnp.full((1, 1), lg_ref[h], F32))
        q = _rotate(q_ref[:, h * dk:(h + 1) * dk].astype(F32), cos, sin)
        k = _rotate(k_ref[:, h * dk:(h + 1) * dk].astype(F32), cos, sin) * (dk ** -0.5)
        v = v_ref[:, h * dv:(h + 1) * dv].astype(F32)
        st = st_ref[h]
        q_col = jnp.broadcast_to(q * gamma, (LANES, dk)).T[:, 0:1]
        k_col = jnp.broadcast_to(k, (LANES, dk)).T[:, 0:1]
        qk = jnp.sum(q * k, axis=-1, keepdims=True)
        o = qk * v + jnp.sum(q_col * st, axis=0, keepdims=True)
        so_ref[h] = st * gamma + k_col * v
        o = _rms(o, EPS)
        gate = g_ref[:, h * dv:(h + 1) * dv].astype(F32)
        o_ref[:, h * dv:(h + 1) * dv] = (o * (gate * _sigmoid(gate))).astype(o_ref.dtype)


def _ret_decode(z, col_q, col_k, col_v, col_g, state, cos, sin, lg):
    nb = z.shape[0]
    _, _, dk, dv = state.shape
    z3 = z.reshape(nb, 1, z.shape[1])
    wq, wv = RET_HEADS * dk, RET_HEADS * dv

    def row(width, col):
        return pl.BlockSpec((None, 1, width), lambda b: (b, 0, col // width))

    o, st = pl.pallas_call(
        _ret_decode_kernel,
        grid=(nb,),
        in_specs=[pl.BlockSpec(memory_space=pltpu.SMEM),
                  row(wq, col_q), row(wq, col_k), row(wv, col_v), row(wv, col_g),
                  pl.BlockSpec((1, dk // 2), lambda b: (0, 0)),
                  pl.BlockSpec((1, dk // 2), lambda b: (0, 0)),
                  pl.BlockSpec((None, RET_HEADS, dk, dv), lambda b: (b, 0, 0, 0))],
        out_specs=[pl.BlockSpec((None, 1, wv), lambda b: (b, 0, 0)),
                   pl.BlockSpec((None, RET_HEADS, dk, dv), lambda b: (b, 0, 0, 0))],
        out_shape=[jax.ShapeDtypeStruct((nb, 1, wv), BF16),
                   jax.ShapeDtypeStruct(state.shape, F32)],
        compiler_params=_cparams(("arbitrary",)),
        name="retention_decode",
    )(lg, z3, z3, z3, z3, cos, sin, state)
    return o.reshape(nb, wv), st


def _postmix_kernel(oa_ref, ob_ref, ga_ref, gb_ref, x_ref, gt1_ref, sc2_ref, sh2_ref,
                    wa_ref, wb_ref, wo_ref, gpost_ref, gpre_ref, wr_ref, br_ref,
                    x1_ref, h2_ref, ti_ref, tg_ref, cnt_ref, run_cnt):
    a = _dot(oa_ref[...], wa_ref[...])
    b = _dot(ob_ref[...], wb_ref[...])
    merged = _sigmoid(ga_ref[...].astype(F32)) * a + _sigmoid(gb_ref[...].astype(F32)) * b
    y = _dot(merged.astype(BF16), wo_ref[...])
    x1 = x_ref[...] + gt1_ref[...] * (_rms(y, EPS) * gpost_ref[...])
    x1_ref[...] = x1
    h2 = _rms(x1, EPS) * gpre_ref[...] * (1.0 + sc2_ref[...]) + sh2_ref[...]
    h2_ref[...] = h2

    logits = jnp.dot(h2, wr_ref[...], preferred_element_type=F32,
                     precision=lax.Precision.HIGHEST) + br_ref[...]
    lane = lax.broadcasted_iota(jnp.int32, logits.shape, 1)
    cur = logits
    vals, idxs = [], []
    for _ in range(TOP_K):
        m = jnp.max(cur, axis=-1, keepdims=True)
        idx = jnp.min(jnp.where(cur == m, lane, LANES), axis=-1, keepdims=True)
        vals.append(m)
        idxs.append(idx)
        cur = jnp.where(lane == idx, NEG * 2.0, cur)
    exps = [jnp.exp(v - vals[0]) for v in vals]
    denom = exps[0]
    for e in exps[1:]:
        denom = denom + e
    @pl.when(pl.program_id(0) == 0)
    def _():
        run_cnt[...] = jnp.zeros(run_cnt.shape, F32)

    chosen = jnp.zeros(logits.shape, F32)
    for kk in range(TOP_K):
        chosen = chosen + jnp.where(lane == idxs[kk], 1.0, 0.0)
    tm = logits.shape[0]
    earlier = lax.broadcasted_iota(jnp.int32, (tm, tm), 0) > lax.broadcasted_iota(jnp.int32, (tm, tm), 1)
    before = _dot(jnp.where(earlier, 1.0, 0.0).astype(BF16), chosen.astype(BF16)) + run_cnt[...]
    ti = jnp.zeros(logits.shape, jnp.int32)
    tg = jnp.zeros(logits.shape, F32)
    for kk in range(TOP_K):
        rank = jnp.sum(jnp.where(lane == idxs[kk], before, 0.0), axis=-1, keepdims=True)
        ti = jnp.where(lane == kk, idxs[kk], ti)
        ti = jnp.where(lane == TOP_K + kk, rank.astype(jnp.int32), ti)
        tg = jnp.where(lane == kk, exps[kk] / denom, tg)
    ti_ref[...] = ti
    tg_ref[...] = tg
    run_cnt[...] = run_cnt[...] + jnp.sum(chosen, axis=0, keepdims=True)
    cnt_ref[...] = run_cnt[...].astype(jnp.int32)


def _postmix(o_a, o_b, z, col_ga, col_gb, x, mod, wa, wb, wo, gpost, gpre, wr, br):
    n, d = x.shape
    tm = mod.tm
    da, db = o_a.shape[1], o_b.shape[1]
    const = lambda shape: pl.BlockSpec(shape, lambda i: (0, 0))
    rows = lambda w: pl.BlockSpec((tm, w), lambda i: (i, 0))
    return pl.pallas_call(
        _postmix_kernel,
        grid=(n // tm,),
        in_specs=[rows(da), rows(db),
                  pl.BlockSpec((tm, d), lambda i: (i, col_ga // d)),
                  pl.BlockSpec((tm, d), lambda i: (i, col_gb // d)),
                  rows(d), mod.spec(2), mod.spec(4), mod.spec(3),
                  const((da, d)), const((db, d)), const((d, d)),
                  const((1, d)), const((1, d)), const((d, LANES)), const((1, LANES))],
        out_specs=[rows(d), rows(d), rows(LANES), rows(LANES), const((1, LANES))],
        out_shape=[jax.ShapeDtypeStruct((n, d), F32), jax.ShapeDtypeStruct((n, d), F32),
                   jax.ShapeDtypeStruct((n, LANES), jnp.int32),
                   jax.ShapeDtypeStruct((n, LANES), F32),
                   jax.ShapeDtypeStruct((1, LANES), jnp.int32)],
        scratch_shapes=[pltpu.VMEM((1, LANES), F32)],
        compiler_params=_cparams(("arbitrary",)),
        name="postmix_router",
    )(o_a, o_b, z, z, x, mod.arr, mod.arr, mod.arr, wa, wb, wo,
      gpost.reshape(1, d), gpre.reshape(1, d), wr, br)


def _row_copy(src_hbm, row, dst, r, sem):
    return pltpu.make_async_copy(src_hbm.at[pl.ds(row, 1)], dst.at[pl.ds(r, 1)], sem)


def _dispatch_kernel(pos_ref, start_ref, cnt_ref, h_ref, o_hbm, zeros, sem, *, tm, tm_ffn, n_experts):
    i = pl.program_id(0)
    base = i * tm * TOP_K

    def slot_copy(src, r, slot):
        return pltpu.make_async_copy(src.at[pl.ds(r, 1)], o_hbm.at[pl.ds(slot, 1)], sem)

    def issue(r, carry):
        for kk in range(TOP_K):
            slot_copy(h_ref, r, pos_ref[base + r * TOP_K + kk]).start()
        return carry

    lax.fori_loop(0, tm, issue, 0, unroll=4)

    def wait(r, carry):
        for kk in range(TOP_K):
            slot_copy(h_ref, r, 0).wait()
        return carry

    lax.fori_loop(0, tm, wait, 0, unroll=4)

    @pl.when(i == pl.num_programs(0) - 1)
    def _():
        zeros[...] = jnp.zeros(zeros.shape, zeros.dtype)

        def per_expert(e, carry):
            cnt = cnt_ref[e]
            first = start_ref[e] + cnt
            n_pad = (tm_ffn - cnt % tm_ffn) % tm_ffn

            def pad(j, c):
                slot_copy(zeros, 0, first + j).start()
                return c

            def pad_wait(j, c):
                slot_copy(zeros, 0, 0).wait()
                return c

            lax.fori_loop(0, n_pad, pad, 0)
            lax.fori_loop(0, n_pad, pad_wait, 0)
            return carry

        lax.fori_loop(0, n_experts, per_expert, 0)

        last = n_experts - 1
        used = start_ref[last] + (cnt_ref[last] + tm_ffn - 1) // tm_ffn * tm_ffn

        def tile_copy(t):
            row0 = pl.multiple_of(used + t * tm_ffn, tm_ffn)
            return pltpu.make_async_copy(zeros, o_hbm.at[pl.ds(row0, tm_ffn)], sem)

        def fill(t, c):
            tile_copy(t).start()
            return c

        def fill_wait(t, c):
            tile_copy(t).wait()
            return c

        n_tail = (o_hbm.shape[0] - used) // tm_ffn
        lax.fori_loop(0, n_tail, fill, 0)
        lax.fori_loop(0, n_tail, fill_wait, 0)


def _dispatch_rows(h, pos, pstart, counts, n_slots, tm, tm_ffn):
    n, d = h.shape
    grid_spec = pltpu.PrefetchScalarGridSpec(
        num_scalar_prefetch=3,
        grid=(n // tm,),
        in_specs=[pl.BlockSpec((tm, d), lambda i, *_: (i, 0))],
        out_specs=pl.BlockSpec(memory_space=pl.ANY),
        scratch_shapes=[pltpu.VMEM((tm_ffn, d), h.dtype), pltpu.SemaphoreType.DMA(())])
    return pl.pallas_call(
        functools.partial(_dispatch_kernel, tm=tm, tm_ffn=tm_ffn, n_experts=pstart.shape[0]),
        grid_spec=grid_spec,
        out_shape=jax.ShapeDtypeStruct((n_slots, d), h.dtype),
        compiler_params=_cparams(("arbitrary",)),
        name="moe_dispatch",
    )(pos, pstart, counts, h)


def _ffn_kernel(te_ref, nv_ref, x_ref, wg_ref, wl_ref, bg_ref, bl_ref, wd_ref, bd_ref, o_ref):
    del te_ref
    t = pl.program_id(0)

    @pl.when(t < nv_ref[0])
    def _():
        x = x_ref[...].astype(BF16)
        g = jnp.minimum(_dot(x, wg_ref[...]) + bg_ref[...], SWIGLU_LIMIT)
        l = jnp.clip(_dot(x, wl_ref[...]) + bl_ref[...], -SWIGLU_LIMIT, SWIGLU_LIMIT)
        act = g * _sigmoid(SWIGLU_ALPHA * g) * (l + 1.0)
        o_ref[...] = _dot(act.astype(BF16), wd_ref[...]) + bd_ref[...]

    @pl.when(t >= nv_ref[0])
    def _():
        o_ref[...] = jnp.zeros(o_ref.shape, o_ref.dtype)


def _expert_ffn(xg, tile_expert, n_valid, w_up, b_up, w_dn, b_dn, tm):
    p, d = xg.shape
    dff = w_up.shape[-1]
    e = w_up.shape[0]
    grid_spec = pltpu.PrefetchScalarGridSpec(
        num_scalar_prefetch=2,
        grid=(p // tm,),
        in_specs=[pl.BlockSpec((tm, d), lambda t, te, nv: (jnp.minimum(t, nv[0] - 1), 0)),
                  pl.BlockSpec((None, None, d, dff), lambda t, te, nv: (te[t], 0, 0, 0)),
                  pl.BlockSpec((None, None, d, dff), lambda t, te, nv: (te[t], 1, 0, 0)),
                  pl.BlockSpec((None, None, 1, dff), lambda t, te, nv: (te[t], 0, 0, 0)),
                  pl.BlockSpec((None, None, 1, dff), lambda t, te, nv: (te[t], 1, 0, 0)),
                  pl.BlockSpec((None, dff, d), lambda t, te, nv: (te[t], 0, 0)),
                  pl.BlockSpec((None, 1, d), lambda t, te, nv: (te[t], 0, 0))],
        out_specs=pl.BlockSpec((tm, d), lambda t, te, nv: (t, 0)))
    return pl.pallas_call(
        _ffn_kernel,
        grid_spec=grid_spec,
        out_shape=jax.ShapeDtypeStruct((p, d), F32),
        compiler_params=_cparams(("arbitrary",)),
        name="moe_expert_ffn",
    )(tile_expert, n_valid, xg, w_up, w_up, b_up.reshape(e, 2, 1, dff), b_up.reshape(e, 2, 1, dff),
      w_dn, b_dn.reshape(e, 1, d))


def _combine_kernel(pos_ref, y_hbm, tg_ref, x1_ref, gt2_ref, gpost_ref, o_ref, buf, sem, *, tm):
    base = pl.program_id(0) * tm * TOP_K

    def issue(r, carry):
        for kk in range(TOP_K):
            _row_copy(y_hbm, pos_ref[base + r * TOP_K + kk], buf.at[kk], r, sem).start()
        return carry

    lax.fori_loop(0, tm, issue, 0, unroll=4)

    def wait(r, carry):
        for kk in range(TOP_K):
            _row_copy(y_hbm, 0, buf.at[kk], r, sem).wait()
        return carry

    lax.fori_loop(0, tm, wait, 0, unroll=4)

    tg = tg_ref[...]
    f = tg[:, 0:1] * buf[0]
    for kk in range(1, TOP_K):
        f = f + tg[:, kk:kk + 1] * buf[kk]
    o_ref[...] = x1_ref[...] + gt2_ref[...] * (_rms(f, EPS) * gpost_ref[...])


def _combine(ys, pos, tg, x1, mod, gpost):
    n, d = x1.shape
    tm = mod.tm
    grid_spec = pltpu.PrefetchScalarGridSpec(
        num_scalar_prefetch=1,
        grid=(n // tm,),
        in_specs=[pl.BlockSpec(memory_space=pl.ANY),
                  pl.BlockSpec((tm, LANES), lambda i, pos: (i, 0)),
                  pl.BlockSpec((tm, d), lambda i, pos: (i, 0)),
                  _with_prefetch(mod.spec(5)),
                  pl.BlockSpec((1, d), lambda i, pos: (0, 0))],
        out_specs=pl.BlockSpec((tm, d), lambda i, pos: (i, 0)),
        scratch_shapes=[pltpu.VMEM((TOP_K, tm, d), F32), pltpu.SemaphoreType.DMA(())])
    return pl.pallas_call(
        functools.partial(_combine_kernel, tm=tm),
        grid_spec=grid_spec,
        out_shape=jax.ShapeDtypeStruct((n, d), F32),
        compiler_params=_cparams(("arbitrary",)),
        name="moe_combine",
    )(pos, ys, tg, x1, mod.arr, gpost.reshape(1, d))


def _with_prefetch(spec):
    inner = spec.index_map
    return pl.BlockSpec(spec.block_shape, lambda i, _s: inner(i))


def _route(top_i, rank, counts, tm):
    n = top_i.shape[0]
    n_experts = counts.shape[0]
    pcounts = (counts + tm - 1) // tm * tm
    pend = jnp.cumsum(pcounts)
    pstart = pend - pcounts
    experts = jnp.arange(n_experts, dtype=jnp.int32)
    pos = jnp.sum(jnp.where(top_i[:, :, None] == experts, pstart, 0), axis=-1) + rank
    n_slots = (n * TOP_K + n_experts * (tm - 1) + tm - 1) // tm * tm
    tiles = jnp.arange(n_slots // tm, dtype=jnp.int32) * tm
    tile_expert = jnp.minimum(jnp.sum((pend[None, :] <= tiles[:, None]).astype(jnp.int32), axis=1),
                              n_experts - 1)
    n_valid = (pend[-1] // tm).reshape(1)
    return pos.reshape(-1), pstart, tile_expert, n_valid, n_slots


def _moe(h2, top_i, rank, counts, tg, x1, mod, gpost, w_up, b_up, w_dn, b_dn, tm_ffn):
    pos, pstart, tile_expert, n_valid, n_slots = _route(top_i, rank, counts, tm_ffn)
    xg = _dispatch_rows(h2, pos, pstart, counts, n_slots, _pick(h2.shape[0], 512), tm_ffn)
    ys = _expert_ffn(xg, tile_expert, n_valid, w_up, b_up, w_dn, b_dn, tm_ffn)
    return _combine(ys, pos, tg, x1, mod, gpost)


def _layer(x, mod, lw, attend, retain, d):
    tm = mod.tm
    cq, ck, cv = 0, d, 2 * d
    rq, rk, rv, rg = 3 * d, 4 * d, 5 * d, 7 * d
    ga, gb = 9 * d, 10 * d
    h = _prenorm(x, lw['g_pre1'], mod, 1, 0)
    tm_mm = _pick(x.shape[0], 512)
    dq, = _matmul(h, lw['w_in'], cq, d, (BF16,), tm_mm)
    k32, k16 = _matmul(h, lw['w_in'], ck, d, (F32, BF16), tm_mm)
    v32, v16 = _matmul(h, lw['w_in'], cv, d, (F32, BF16), tm_mm)
    z, = _matmul(h, lw['w_in'], rq, 8 * d, (BF16,), tm_mm)
    o_a = attend(dq, k16, v16)
    o_b, state = retain(z, rq - rq, rk - rq, rv - rq, rg - rq)
    x1, h2, routed, tg, counts = _postmix(o_a, o_b, z, ga - rq, gb - rq, x, mod,
                                          lw['w_a'], lw['w_b'], lw['w_o'], lw['g_post1'], lw['g_pre2'],
                                          lw['w_r'], lw['b_r'])
    n_experts = lw['w_up'].shape[0]
    y = _moe(h2, routed[:, :TOP_K], routed[:, TOP_K:2 * TOP_K], counts[0, :n_experts], tg, x1, mod,
             lw['g_post2'], lw['w_up'], lw['b_up'], lw['w_dn'], lw['b_dn'], lw['tm_ffn'])
    return y, k32, v32, state


def _pick(n, pref):
    t = min(n, pref)
    while n % t:
        t //= 2
    return t


def kernel(x_prompt, x_sample, cache_k, cache_v, state_ret, page_table, c_prompt, c_sample, w_ada, b_ada, norm_pre_mix, norm_post_mix, norm_pre_ffn, norm_post_ffn, w_in, lambda_q1, lambda_k1, lambda_q2, lambda_k2, diff_subln, w_branch_a, w_branch_b, w_out, w_router, b_router, w_exp_up, b_exp_up, w_exp_down, b_exp_down):
    batch, seq, d = x_prompt.shape
    nb, dec_seq, _ = x_sample.shape
    assert dec_seq == 1
    depth = w_ada.shape[0]
    n_phys = cache_k.shape[1]
    n_pages = page_table.shape[1]
    past = n_pages * PAGE_SIZE
    n_experts = w_router.shape[-1]
    dff = w_exp_down.shape[-2]
    dk = d // RET_HEADS

    inv = ROPE_BASE ** (-jnp.arange(dk // 2, dtype=F32) * 2.0 / dk)
    ang_p = jnp.arange(seq, dtype=jnp.int32).astype(F32)[:, None] * inv[None, :]
    ang_s = jnp.full((1, 1), past, jnp.int32).astype(F32) * inv[None, :]
    cos_p, sin_p, cos_s, sin_s = jnp.cos(ang_p), jnp.sin(ang_p), jnp.cos(ang_s), jnp.sin(ang_s)
    lg = jnp.log(1.0 - 2.0 ** (-5.0 - jnp.arange(RET_HEADS, dtype=F32)))

    cache_k2 = cache_k.reshape(depth * n_phys, PAGE_SIZE * DIFF_HEADS, HEAD_W)
    cache_v2 = cache_v.reshape(depth * n_phys, PAGE_SIZE * DIFF_HEADS, HEAD_W)

    tm_p = _pick(batch * seq, 256)
    t_attn = _pick(seq, 512)
    c_ret = _pick(seq, 256)
    pages_per_step = _pick(n_pages, 8)

    yp = x_prompt.reshape(batch * seq, d)
    ys = x_sample.reshape(nb, d)
    c_all = jnp.concatenate([c_prompt, c_sample], axis=0)
    pad = (-c_all.shape[0]) % 8
    c_all = jnp.pad(c_all, ((0, pad), (0, 0)))
    outs = [[] for _ in range(6)]
    for l in range(depth):
        lam_init = 0.8 - 0.6 * math.exp(-0.3 * l)
        lams = tuple(a[l].reshape(1, DIFF_HD) for a in (lambda_q1, lambda_k1, lambda_q2, lambda_k2))
        table = _ada(c_all, w_ada[l], b_ada[l])
        mod_p = _Mod(table[:batch], d, tm_p, seq)
        mod_s = _Mod(table[batch:batch + nb], d, nb, 1)
        w_up = w_exp_up[l].reshape(n_experts, d, dff, 2).transpose(0, 3, 1, 2).astype(BF16)
        b_up = b_exp_up[l].reshape(n_experts, dff, 2).transpose(0, 2, 1)
        w_r = jnp.pad(w_router[l], ((0, 0), (0, LANES - n_experts)))
        b_r = jnp.pad(b_router[l], (0, LANES - n_experts), constant_values=NEG).reshape(1, LANES)
        lw = dict(g_pre1=norm_pre_mix[l], g_post1=norm_post_mix[l], g_pre2=norm_pre_ffn[l],
                  g_post2=norm_post_ffn[l], w_in=w_in[l],
                  w_a=w_branch_a[l].astype(BF16), w_b=w_branch_b[l].astype(BF16),
                  w_o=w_out[l].astype(BF16), w_r=w_r, b_r=b_r,
                  w_up=w_up, b_up=b_up, w_dn=w_exp_down[l].astype(BF16), b_dn=b_exp_down[l])

        attend_p = lambda q, k, v: _attn_prompt(q, k, v, lams, diff_subln[l], batch, seq, lam_init, t_attn)
        retain_p = lambda z, a, b, c, g: _ret_prompt(z, a, b, c, g, batch, seq, cos_p, sin_p, lg, c_ret)
        yp, kp, vp, rp = _layer(yp, mod_p, dict(lw, tm_ffn=_pick(batch * seq, 256)), attend_p, retain_p, d)

        attend_s = lambda q, k, v: _attn_decode(q, k, v, cache_k2, cache_v2, l * n_phys, page_table,
                                                lams, diff_subln[l], lam_init, pages_per_step)
        retain_s = lambda z, a, b, c, g: _ret_decode(z, a, b, c, g, state_ret[l], cos_s, sin_s, lg)
        ys, ks, vs, rs = _layer(ys, mod_s, dict(lw, tm_ffn=16), attend_s, retain_s, d)

        for lst, val in zip(outs, (kp.reshape(batch, seq, DIFF_HEADS, HEAD_W),
                                   vp.reshape(batch, seq, DIFF_HEADS, HEAD_W),
                                   ks.reshape(nb, 1, DIFF_HEADS, HEAD_W),
                                   vs.reshape(nb, 1, DIFF_HEADS, HEAD_W), rp, rs)):
            lst.append(val)
    return (yp.reshape(batch, seq, d), ys.reshape(nb, 1, d)) + tuple(jnp.stack(o) for o in outs)
```

```python
import functools
import math

import jax
import jax.numpy as jnp
from jax import lax
from jax.experimental import pallas as pl
from jax.experimental.pallas import tpu as pltpu

F32 = jnp.float32
BF16 = jnp.bfloat16

DIFF_HEADS = 8
DIFF_HD = 64
HEAD_W = 2 * DIFF_HD
RET_HEADS = 4
PAGE_SIZE = 128
TOP_K = 4
ROPE_BASE = 10000.0
SWIGLU_LIMIT = 7.0
SWIGLU_ALPHA = 1.702
EPS = 1e-6
SUBLN_EPS = 1e-5
LANES = 128
VMEM_LIMIT = 52 * 1024 * 1024
NEG = -1e30
ATTN_ROW_GROUP = 128
ATTN_UNROLL = 4

_NT = (((1,), (1,)), ((), ()))


def _cparams(sem):
    return pltpu.CompilerParams(dimension_semantics=sem, vmem_limit_bytes=VMEM_LIMIT)


def _sigmoid(x):
    return 1.0 / (1.0 + jnp.exp(-x))


def _rms(x, eps):
    return x * lax.rsqrt(jnp.mean(x * x, axis=-1, keepdims=True) + eps)


def _dot(a, b):
    return jnp.dot(a, b, preferred_element_type=F32)


def _ada_kernel(c_ref, w_ref, b_ref, o_ref):
    c = c_ref[...]
    s = (c * _sigmoid(c)).astype(BF16)
    o_ref[...] = _dot(s, w_ref[...].astype(BF16)) + b_ref[...]


def _ada(c, w, b, tn=1024):
    r, d = c.shape
    n = w.shape[1]
    return pl.pallas_call(
        _ada_kernel,
        grid=(n // tn,),
        in_specs=[pl.BlockSpec((r, d), lambda j: (0, 0)),
                  pl.BlockSpec((d, tn), lambda j: (0, j)),
                  pl.BlockSpec((1, tn), lambda j: (0, j))],
        out_specs=pl.BlockSpec((r, tn), lambda j: (0, j)),
        out_shape=jax.ShapeDtypeStruct((r, n), F32),
        compiler_params=_cparams(("arbitrary",)),
        name="ada_mod",
    )(c, w, b.reshape(1, n))


class _Mod:
    def __init__(self, table, d, tm, rows_per_entry):
        self.d = d
        self.tm = tm
        if rows_per_entry == 1:
            self.arr = table
            self.per_row = True
        else:
            self.arr = table.reshape(table.shape[0], 1, table.shape[1])
            self.per_row = False
            self.tiles_per_entry = rows_per_entry // tm

    def spec(self, c):
        if self.per_row:
            return pl.BlockSpec((self.tm, self.d), lambda i: (i, c))
        tpe = self.tiles_per_entry
        return pl.BlockSpec((None, 1, self.d), lambda i: (i // tpe, 0, c))


def _prenorm_kernel(x_ref, g_ref, sc_ref, sh_ref, o_ref):
    y = _rms(x_ref[...], EPS) * g_ref[...]
    o_ref[...] = (y * (1.0 + sc_ref[...]) + sh_ref[...]).astype(o_ref.dtype)


def _prenorm(x, g, mod, c_scale, c_shift):
    n, d = x.shape
    tm = mod.tm
    return pl.pallas_call(
        _prenorm_kernel,
        grid=(n // tm,),
        in_specs=[pl.BlockSpec((tm, d), lambda i: (i, 0)),
                  pl.BlockSpec((1, d), lambda i: (0, 0)),
                  mod.spec(c_scale), mod.spec(c_shift)],
        out_specs=pl.BlockSpec((tm, d), lambda i: (i, 0)),
        out_shape=jax.ShapeDtypeStruct((n, d), BF16),
        compiler_params=_cparams(("arbitrary",)),
        name="prenorm",
    )(x, g.reshape(1, d), mod.arr, mod.arr)


def _mm_kernel(h_ref, w_ref, *rest):
    outs, wb = rest[:-1], rest[-1]

    @pl.when(pl.program_id(1) == 0)
    def _():
        wb[...] = w_ref[...].astype(BF16)

    acc = _dot(h_ref[...], wb[...])
    for o in outs:
        o[...] = acc.astype(o.dtype)


def _matmul(h, w, col0, ncols, out_dtypes, tm, tn=1024):
    n, k = h.shape
    off = col0 // tn
    outs = pl.pallas_call(
        _mm_kernel,
        grid=(ncols // tn, n // tm),
        in_specs=[pl.BlockSpec((tm, k), lambda j, i: (i, 0)),
                  pl.BlockSpec((k, tn), lambda j, i: (0, j + off))],
        out_specs=[pl.BlockSpec((tm, tn), lambda j, i: (i, j)) for _ in out_dtypes],
        out_shape=[jax.ShapeDtypeStruct((n, ncols), dt) for dt in out_dtypes],
        scratch_shapes=[pltpu.VMEM((k, tn), BF16)],
        compiler_params=_cparams(("arbitrary", "arbitrary")),
        name="in_proj",
    )(h, w)
    return outs


def _lambda(lq1, lk1, lq2, lk2, lam_init):
    a = jnp.sum(lq1[...] * lk1[...], axis=-1, keepdims=True)
    b = jnp.sum(lq2[...] * lk2[...], axis=-1, keepdims=True)
    return jnp.exp(a) - jnp.exp(b) + lam_init


def _stack_components(q):
    lane = lax.broadcasted_iota(jnp.int32, q.shape, 1)
    zero = jnp.zeros_like(q)
    return jnp.concatenate([jnp.where(lane < DIFF_HD, q, zero),
                            jnp.where(lane >= DIFF_HD, q, zero)], axis=0)


def _attn_prompt_kernel(lq1, lk1, lq2, lk2, q_ref, k_ref, v_ref, sub_ref, o_ref,
                        qs, vext, m_s, acc, *, t, lam_init):
    qi = pl.program_id(2)

    @pl.when(qi == 0)
    def _():
        vext[:, :HEAD_W] = v_ref[...]
        vext[:, HEAD_W:] = jnp.ones((vext.shape[0], HEAD_W), vext.dtype)

    qs[...] = _stack_components(q_ref[...].astype(F32) * (DIFF_HD ** -0.5)).astype(BF16)
    m_s[...] = jnp.full(m_s.shape, -jnp.inf, F32)
    acc[...] = jnp.zeros(acc.shape, F32)

    def step(ki, masked):
        start = pl.multiple_of(ki * t, t)
        k = k_ref[pl.ds(start, t), :]
        v = vext[pl.ds(start, t), :]
        group = min(ATTN_ROW_GROUP, t)
        for g0 in range(0, 2 * t, group):
            rows = pl.ds(g0, group)
            s = lax.dot_general(qs[rows, :], k, _NT, preferred_element_type=F32)
            if masked:
                row = lax.broadcasted_iota(jnp.int32, s.shape, 0) + (g0 % t)
                col = lax.broadcasted_iota(jnp.int32, s.shape, 1)
                s = jnp.where(col <= row, s, -jnp.inf)
            chunks = [s[:, j * LANES:(j + 1) * LANES] for j in range(t // LANES)]
            cmax = chunks[0]
            for c in chunks[1:]:
                cmax = jnp.maximum(cmax, c)
            m_old = m_s[rows, :]
            m_new = jnp.maximum(m_old, jnp.broadcast_to(jnp.max(cmax, axis=-1, keepdims=True), m_old.shape))
            alpha = jnp.exp(m_old - m_new)
            p = jnp.concatenate([jnp.exp(c - m_new).astype(BF16) for c in chunks], axis=1)
            acc[rows, :] = jnp.concatenate([alpha, alpha], axis=1) * acc[rows, :] + _dot(p, v)
            m_s[rows, :] = m_new

    def body(i, carry):
        for u in range(ATTN_UNROLL):
            step(ATTN_UNROLL * i + u, False)
        return carry

    lax.fori_loop(0, qi // ATTN_UNROLL, body, 0)

    rem = qi % ATTN_UNROLL
    bit = ATTN_UNROLL // 2
    while bit >= 1:
        def run(bit=bit):
            first = qi - rem % (2 * bit)
            for j in range(bit):
                step(first + j, False)
        pl.when((rem & bit) != 0)(run)
        bit //= 2

    step(qi, True)

    lam = _lambda(lq1, lk1, lq2, lk2, lam_init)
    a = acc[...]
    o = a[:, :HEAD_W] / a[:, HEAD_W:]
    o = o[:t] - lam * o[t:]
    o = _rms(o, SUBLN_EPS) * sub_ref[...] * (1.0 - lam_init)
    o_ref[...] = o.astype(o_ref.dtype)


def _attn_prompt(q, k, v, lams, subln, batch, seq, lam_init, t):
    n, w = q.shape
    nq = seq // t
    lam_specs = [pl.BlockSpec((1, DIFF_HD), lambda b, h, i: (0, 0)) for _ in range(4)]
    return pl.pallas_call(
        functools.partial(_attn_prompt_kernel, t=t, lam_init=lam_init),
        grid=(batch, DIFF_HEADS, nq),
        in_specs=lam_specs + [
            pl.BlockSpec((t, HEAD_W), lambda b, h, i: (b * nq + i, h)),
            pl.BlockSpec((seq, HEAD_W), lambda b, h, i: (b, h)),
            pl.BlockSpec((seq, HEAD_W), lambda b, h, i: (b, h)),
            pl.BlockSpec((1, HEAD_W), lambda b, h, i: (0, 0))],
        out_specs=pl.BlockSpec((t, HEAD_W), lambda b, h, i: (b * nq + i, h)),
        out_shape=jax.ShapeDtypeStruct((n, w), BF16),
        scratch_shapes=[pltpu.VMEM((2 * t, HEAD_W), BF16),
                        pltpu.VMEM((seq, 2 * HEAD_W), BF16),
                        pltpu.VMEM((2 * t, HEAD_W), F32),
                        pltpu.VMEM((2 * t, 2 * HEAD_W), F32)],
        compiler_params=_cparams(("arbitrary", "arbitrary", "arbitrary")),
        name="diff_attn_prompt",
    )(*lams, q, k, v, subln.reshape(1, HEAD_W))


def _attn_decode_kernel(pt_ref, lq1, lk1, lq2, lk2, q_ref, kn_ref, vn_ref, sub_ref, k_hbm, v_hbm,
                        o_ref, kbuf, vbuf, sem, qm, bias, m_s, l_s, acc, *, pages_per_step, lam_init):
    g = pages_per_step
    p_idx = pl.program_id(1)
    n_steps = pl.num_programs(1)
    lin = pl.program_id(0) * n_steps + p_idx
    slot = lin % 2

    def page_copies(step, slot):
        out = []
        for r in range(g):
            page = pt_ref[step * g + r]
            out.append(pltpu.make_async_copy(k_hbm.at[page], kbuf.at[slot, r], sem.at[slot, 0]))
            out.append(pltpu.make_async_copy(v_hbm.at[page], vbuf.at[slot, r], sem.at[slot, 1]))
        return out

    @pl.when(lin == 0)
    def _():
        for c in page_copies(0, 0):
            c.start()

    @pl.when(lin + 1 < pl.num_programs(0) * n_steps)
    def _():
        for c in page_copies(lin + 1, 1 - slot):
            c.start()

    for c in page_copies(lin, slot):
        c.wait()

    @pl.when(p_idx == 0)
    def _():
        qm[...] = _stack_components(q_ref[...] * (DIFF_HD ** -0.5)).astype(BF16)
        row = lax.broadcasted_iota(jnp.int32, bias.shape, 0)
        col = lax.broadcasted_iota(jnp.int32, bias.shape, 1)
        bias[...] = jnp.where(row % DIFF_HEADS == col % DIFF_HEADS, 0.0, -jnp.inf)
        m_s[...] = jnp.full(m_s.shape, -jnp.inf, F32)
        l_s[...] = jnp.zeros(l_s.shape, F32)
        acc[...] = jnp.zeros(acc.shape, F32)

    scores = [lax.dot_general(qm[...], kbuf[slot, r].astype(BF16), _NT, preferred_element_type=F32)
              + bias[...] for r in range(g)]
    m_new = m_s[...]
    for s in scores:
        m_new = jnp.maximum(m_new, jnp.max(s, axis=-1, keepdims=True))
    alpha = jnp.exp(m_s[...] - m_new)
    l_new = alpha * l_s[...]
    a_new = alpha * acc[...]
    for r, s in enumerate(scores):
        p = jnp.exp(s - m_new)
        l_new = l_new + jnp.sum(p, axis=-1, keepdims=True)
        a_new = a_new + _dot(p.astype(BF16), vbuf[slot, r].astype(BF16))
    m_s[...] = m_new
    l_s[...] = l_new
    acc[...] = a_new

    @pl.when(p_idx == pl.num_programs(1) - 1)
    def _():
        kn = jnp.concatenate([kn_ref[...], kn_ref[...]], axis=0)
        vn = jnp.concatenate([vn_ref[...], vn_ref[...]], axis=0)
        s = jnp.sum(qm[...].astype(F32) * kn, axis=-1, keepdims=True)
        m_fin = jnp.maximum(m_new, s)
        beta = jnp.exp(m_new - m_fin)
        p = jnp.exp(s - m_fin)
        o = (beta * a_new + p * vn) / (beta * l_new + p)
        lam = _lambda(lq1, lk1, lq2, lk2, lam_init)
        o = o[:DIFF_HEADS] - lam * o[DIFF_HEADS:]
        o_ref[...] = _rms(o, SUBLN_EPS) * sub_ref[...] * (1.0 - lam_init)


def _attn_decode(q, k_new, v_new, cache_k, cache_v, page_base, page_table, lams, subln,
                 lam_init, pages_per_step):
    nb, d = q.shape
    n_pages = page_table.shape[1]
    g = pages_per_step
    steps = n_pages // g
    rows = PAGE_SIZE * DIFF_HEADS
    pt_flat = (page_table + page_base).reshape(-1).astype(jnp.int32)

    head_spec = pl.BlockSpec((None, DIFF_HEADS, HEAD_W), lambda b, p, pt: (b, 0, 0))
    lam_specs = [pl.BlockSpec((1, DIFF_HD), lambda b, p, pt: (0, 0)) for _ in range(4)]
    grid_spec = pltpu.PrefetchScalarGridSpec(
        num_scalar_prefetch=1,
        grid=(nb, steps),
        in_specs=lam_specs + [head_spec, head_spec, head_spec,
                              pl.BlockSpec((1, HEAD_W), lambda b, p, pt: (0, 0))]
        + [pl.BlockSpec(memory_space=pl.ANY), pl.BlockSpec(memory_space=pl.ANY)],
        out_specs=head_spec,
        scratch_shapes=[pltpu.VMEM((2, g, rows, HEAD_W), F32),
                        pltpu.VMEM((2, g, rows, HEAD_W), F32),
                        pltpu.SemaphoreType.DMA((2, 2)),
                        pltpu.VMEM((2 * DIFF_HEADS, HEAD_W), BF16),
                        pltpu.VMEM((2 * DIFF_HEADS, rows), F32),
                        pltpu.VMEM((2 * DIFF_HEADS, 1), F32),
                        pltpu.VMEM((2 * DIFF_HEADS, 1), F32),
                        pltpu.VMEM((2 * DIFF_HEADS, HEAD_W), F32)])
    per_head = lambda a: a.astype(F32).reshape(nb, DIFF_HEADS, HEAD_W)
    out = pl.pallas_call(
        functools.partial(_attn_decode_kernel, pages_per_step=g, lam_init=lam_init),
        grid_spec=grid_spec,
        out_shape=jax.ShapeDtypeStruct((nb, DIFF_HEADS, HEAD_W), F32),
        compiler_params=_cparams(("arbitrary", "arbitrary")),
        name="diff_attn_decode",
    )(pt_flat, *lams, per_head(q), per_head(k_new), per_head(v_new),
      subln.reshape(1, HEAD_W), cache_k, cache_v)
    return out.reshape(nb, d).astype(BF16)


def _rotate(x, cos, sin):
    half = x.shape[-1] // 2
    x1, x2 = x[:, :half], x[:, half:]
    return jnp.concatenate([x1 * cos - x2 * sin, x2 * cos + x1 * sin], axis=-1)


def _ret_prompt_kernel(lg_ref, q_ref, k_ref, v_ref, g_ref, cos_ref, sin_ref, o_ref, st_ref,
                       state, dmask, *, c):
    h = pl.program_id(1)
    ci = pl.program_id(2)
    lg = lg_ref[h]
    idx = lax.broadcasted_iota(jnp.int32, (c, 1), 0).astype(F32)

    @pl.when(ci == 0)
    def _():
        state[...] = jnp.zeros(state.shape, F32)
        row = lax.broadcasted_iota(jnp.int32, (c, c), 0)
        col = lax.broadcasted_iota(jnp.int32, (c, c), 1)
        diff = (row - col).astype(F32)
        dmask[...] = jnp.where(diff >= 0, jnp.exp(jnp.maximum(diff, 0.0) * lg), 0.0)

    cos, sin = cos_ref[...], sin_ref[...]
    q = _rotate(q_ref[...].astype(F32), cos, sin)
    k = _rotate(k_ref[...].astype(F32), cos, sin) * (q.shape[-1] ** -0.5)
    v = v_ref[...]
    q_dec = jnp.exp((idx + 1.0) * lg)
    k_dec = jnp.exp((c - 1.0 - idx) * lg)
    c_dec = jnp.exp(jnp.full((1, 1), c * lg, F32))

    s = lax.dot_general(q.astype(BF16), k.astype(BF16), _NT, preferred_element_type=F32) * dmask[...]
    st = state[...]
    o = _dot(s.astype(BF16), v) + _dot((q * q_dec).astype(BF16), st.astype(BF16))
    kd_t = (k * k_dec).T.astype(BF16)
    st = st * c_dec + _dot(kd_t, v)
    state[...] = st

    o = _rms(o, EPS)
    gate = g_ref[...].astype(F32)
    o_ref[...] = (o * (gate * _sigmoid(gate))).astype(o_ref.dtype)

    @pl.when(ci == pl.num_programs(2) - 1)
    def _():
        st_ref[...] = st


def _ret_prompt(z, col_q, col_k, col_v, col_g, batch, seq, cos, sin, lg, c):
    n = z.shape[0]
    dk = (col_k - col_q) // RET_HEADS
    dv = (col_g - col_v) // RET_HEADS
    nc = seq // c
    qo, ko, vo, go = col_q // dk, col_k // dk, col_v // dv, col_g // dv
    grid_spec = pltpu.PrefetchScalarGridSpec(
        num_scalar_prefetch=0,
        grid=(batch, RET_HEADS, nc),
        in_specs=[pl.BlockSpec(memory_space=pltpu.SMEM),
                  pl.BlockSpec((c, dk), lambda b, h, i: (b * nc + i, qo + h)),
                  pl.BlockSpec((c, dk), lambda b, h, i: (b * nc + i, ko + h)),
                  pl.BlockSpec((c, dv), lambda b, h, i: (b * nc + i, vo + h)),
                  pl.BlockSpec((c, dv), lambda b, h, i: (b * nc + i, go + h)),
                  pl.BlockSpec((c, dk // 2), lambda b, h, i: (i, 0)),
                  pl.BlockSpec((c, dk // 2), lambda b, h, i: (i, 0))],
        out_specs=[pl.BlockSpec((c, dv), lambda b, h, i: (b * nc + i, h)),
                   pl.BlockSpec((None, None, dk, dv), lambda b, h, i: (b, h, 0, 0))],
        scratch_shapes=[pltpu.VMEM((dk, dv), F32), pltpu.VMEM((c, c), F32)])
    return pl.pallas_call(
        functools.partial(_ret_prompt_kernel, c=c),
        grid_spec=grid_spec,
        out_shape=[jax.ShapeDtypeStruct((n, RET_HEADS * dv), BF16),
                   jax.ShapeDtypeStruct((batch, RET_HEADS, dk, dv), F32)],
        compiler_params=_cparams(("arbitrary", "arbitrary", "arbitrary")),
        name="retention_prompt",
    )(lg, z, z, z, z, cos, sin)


def _ret_decode_kernel(lg_ref, q_ref, k_ref, v_ref, g_ref, cos_ref, sin_ref, st_ref, o_ref, so_ref):
    dk, dv = st_ref.shape[1], st_ref.shape[2]
    cos, sin = cos_ref[...], sin_ref[...]
    for h in range(RET_HEADS):
        gamma = jnp.exp(jnp.full((1, 1), lg_ref[h], F32))
        q = _rotate(q_ref[:, h * dk:(h + 1) * dk].astype(F32), cos, sin)
        k = _rotate(k_ref[:, h * dk:(h + 1) * dk].astype(F32), cos, sin) * (dk ** -0.5)
        v = v_ref[:, h * dv:(h + 1) * dv].astype(F32)
        st = st_ref[h]
        q_col = jnp.broadcast_to(q * gamma, (LANES, dk)).T[:, 0:1]
        k_col = jnp.broadcast_to(k, (LANES, dk)).T[:, 0:1]
        qk = jnp.sum(q * k, axis=-1, keepdims=True)
        o = qk * v + jnp.sum(q_col * st, axis=0, keepdims=True)
        so_ref[h] = st * gamma + k_col * v
        o = _rms(o, EPS)
        gate = g_ref[:, h * dv:(h + 1) * dv].astype(F32)
        o_ref[:, h * dv:(h + 1) * dv] = (o * (gate * _sigmoid(gate))).astype(o_ref.dtype)


def _ret_decode(z, col_q, col_k, col_v, col_g, state, cos, sin, lg):
    nb = z.shape[0]
    _, _, dk, dv = state.shape
    z3 = z.reshape(nb, 1, z.shape[1])
    wq, wv = RET_HEADS * dk, RET_HEADS * dv

    def row(width, col):
        return pl.BlockSpec((None, 1, width), lambda b: (b, 0, col // width))

    o, st = pl.pallas_call(
        _ret_decode_kernel,
        grid=(nb,),
        in_specs=[pl.BlockSpec(memory_space=pltpu.SMEM),
                  row(wq, col_q), row(wq, col_k), row(wv, col_v), row(wv, col_g),
                  pl.BlockSpec((1, dk // 2), lambda b: (0, 0)),
                  pl.BlockSpec((1, dk // 2), lambda b: (0, 0)),
                  pl.BlockSpec((None, RET_HEADS, dk, dv), lambda b: (b, 0, 0, 0))],
        out_specs=[pl.BlockSpec((None, 1, wv), lambda b: (b, 0, 0)),
                   pl.BlockSpec((None, RET_HEADS, dk, dv), lambda b: (b, 0, 0, 0))],
        out_shape=[jax.ShapeDtypeStruct((nb, 1, wv), BF16),
                   jax.ShapeDtypeStruct(state.shape, F32)],
        compiler_params=_cparams(("arbitrary",)),
        name="retention_decode",
    )(lg, z3, z3, z3, z3, cos, sin, state)
    return o.reshape(nb, wv), st


def _postmix_kernel(oa_ref, ob_ref, ga_ref, gb_ref, x_ref, gt1_ref, sc2_ref, sh2_ref,
                    wa_ref, wb_ref, wo_ref, gpost_ref, gpre_ref, wr_ref, br_ref,
                    x1_ref, h2_ref, ti_ref, tg_ref, cnt_ref, run_cnt):
    a = _dot(oa_ref[...], wa_ref[...])
    b = _dot(ob_ref[...], wb_ref[...])
    merged = _sigmoid(ga_ref[...].astype(F32)) * a + _sigmoid(gb_ref[...].astype(F32)) * b
    y = _dot(merged.astype(BF16), wo_ref[...])
    x1 = x_ref[...] + gt1_ref[...] * (_rms(y, EPS) * gpost_ref[...])
    x1_ref[...] = x1
    h2 = _rms(x1, EPS) * gpre_ref[...] * (1.0 + sc2_ref[...]) + sh2_ref[...]
    h2_ref[...] = h2

    logits = _dot(h2.astype(BF16), wr_ref[...]) + br_ref[...]
    lane = lax.broadcasted_iota(jnp.int32, logits.shape, 1)
    cur = logits
    vals, idxs = [], []
    for _ in range(TOP_K):
        m = jnp.max(cur, axis=-1, keepdims=True)
        idx = jnp.min(jnp.where(cur == m, lane, LANES), axis=-1, keepdims=True)
        vals.append(m)
        idxs.append(idx)
        cur = jnp.where(lane == idx, NEG * 2.0, cur)
    exps = [jnp.exp(v - vals[0]) for v in vals]
    denom = exps[0]
    for e in exps[1:]:
        denom = denom + e
    @pl.when(pl.program_id(0) == 0)
    def _():
        run_cnt[...] = jnp.zeros(run_cnt.shape, F32)

    chosen = jnp.zeros(logits.shape, F32)
    for kk in range(TOP_K):
        chosen = chosen + jnp.where(lane == idxs[kk], 1.0, 0.0)
    tm = logits.shape[0]
    earlier = lax.broadcasted_iota(jnp.int32, (tm, tm), 0) > lax.broadcasted_iota(jnp.int32, (tm, tm), 1)
    before = _dot(jnp.where(earlier, 1.0, 0.0).astype(BF16), chosen.astype(BF16)) + run_cnt[...]
    ti = jnp.zeros(logits.shape, jnp.int32)
    tg = jnp.zeros(logits.shape, F32)
    for kk in range(TOP_K):
        rank = jnp.sum(jnp.where(lane == idxs[kk], before, 0.0), axis=-1, keepdims=True)
        ti = jnp.where(lane == kk, idxs[kk], ti)
        ti = jnp.where(lane == TOP_K + kk, rank.astype(jnp.int32), ti)
        tg = jnp.where(lane == kk, exps[kk] / denom, tg)
    ti_ref[...] = ti
    tg_ref[...] = tg
    run_cnt[...] = run_cnt[...] + jnp.sum(chosen, axis=0, keepdims=True)
    cnt_ref[...] = run_cnt[...].astype(jnp.int32)


def _postmix(o_a, o_b, z, col_ga, col_gb, x, mod, wa, wb, wo, gpost, gpre, wr, br):
    n, d = x.shape
    tm = mod.tm
    da, db = o_a.shape[1], o_b.shape[1]
    const = lambda shape: pl.BlockSpec(shape, lambda i: (0, 0))
    rows = lambda w: pl.BlockSpec((tm, w), lambda i: (i, 0))
    return pl.pallas_call(
        _postmix_kernel,
        grid=(n // tm,),
        in_specs=[rows(da), rows(db),
                  pl.BlockSpec((tm, d), lambda i: (i, col_ga // d)),
                  pl.BlockSpec((tm, d), lambda i: (i, col_gb // d)),
                  rows(d), mod.spec(2), mod.spec(4), mod.spec(3),
                  const((da, d)), const((db, d)), const((d, d)),
                  const((1, d)), const((1, d)), const((d, LANES)), const((1, LANES))],
        out_specs=[rows(d), rows(d), rows(LANES), rows(LANES), const((1, LANES))],
        out_shape=[jax.ShapeDtypeStruct((n, d), F32), jax.ShapeDtypeStruct((n, d), F32),
                   jax.ShapeDtypeStruct((n, LANES), jnp.int32),
                   jax.ShapeDtypeStruct((n, LANES), F32),
                   jax.ShapeDtypeStruct((1, LANES), jnp.int32)],
        scratch_shapes=[pltpu.VMEM((1, LANES), F32)],
        compiler_params=_cparams(("arbitrary",)),
        name="postmix_router",
    )(o_a, o_b, z, z, x, mod.arr, mod.arr, mod.arr, wa, wb, wo,
      gpost.reshape(1, d), gpre.reshape(1, d), wr, br)


def _row_copy(src_hbm, row, dst, r, sem):
    return pltpu.make_async_copy(src_hbm.at[pl.ds(row, 1)], dst.at[pl.ds(r, 1)], sem)


def _dispatch_kernel(pos_ref, start_ref, cnt_ref, h_ref, o_hbm, zeros, sem, *, tm, tm_ffn, n_experts):
    i = pl.program_id(0)
    base = i * tm * TOP_K

    def slot_copy(src, r, slot):
        return pltpu.make_async_copy(src.at[pl.ds(r, 1)], o_hbm.at[pl.ds(slot, 1)], sem)

    def issue(r, carry):
        for kk in range(TOP_K):
            slot_copy(h_ref, r, pos_ref[base + r * TOP_K + kk]).start()
        return carry

    lax.fori_loop(0, tm, issue, 0, unroll=4)

    def wait(r, carry):
        for kk in range(TOP_K):
            slot_copy(h_ref, r, 0).wait()
        return carry

    lax.fori_loop(0, tm, wait, 0, unroll=4)

    @pl.when(i == pl.num_programs(0) - 1)
    def _():
        zeros[...] = jnp.zeros(zeros.shape, zeros.dtype)

        def per_expert(e, carry):
            cnt = cnt_ref[e]
            first = start_ref[e] + cnt
            n_pad = (tm_ffn - cnt % tm_ffn) % tm_ffn

            def pad(j, c):
                slot_copy(zeros, 0, first + j).start()
                return c

            def pad_wait(j, c):
                slot_copy(zeros, 0, 0).wait()
                return c

            lax.fori_loop(0, n_pad, pad, 0)
            lax.fori_loop(0, n_pad, pad_wait, 0)
            return carry

        lax.fori_loop(0, n_experts, per_expert, 0)

        last = n_experts - 1
        used = start_ref[last] + (cnt_ref[last] + tm_ffn - 1) // tm_ffn * tm_ffn

        def tile_copy(t):
            row0 = pl.multiple_of(used + t * tm_ffn, tm_ffn)
            return pltpu.make_async_copy(zeros, o_hbm.at[pl.ds(row0, tm_ffn)], sem)

        def fill(t, c):
            tile_copy(t).start()
            return c

        def fill_wait(t, c):
            tile_copy(t).wait()
            return c

        n_tail = (o_hbm.shape[0] - used) // tm_ffn
        lax.fori_loop(0, n_tail, fill, 0)
        lax.fori_loop(0, n_tail, fill_wait, 0)


def _dispatch_rows(h, pos, pstart, counts, n_slots, tm, tm_ffn):
    n, d = h.shape
    grid_spec = pltpu.PrefetchScalarGridSpec(
        num_scalar_prefetch=3,
        grid=(n // tm,),
        in_specs=[pl.BlockSpec((tm, d), lambda i, *_: (i, 0))],
        out_specs=pl.BlockSpec(memory_space=pl.ANY),
        scratch_shapes=[pltpu.VMEM((tm_ffn, d), h.dtype), pltpu.SemaphoreType.DMA(())])
    return pl.pallas_call(
        functools.partial(_dispatch_kernel, tm=tm, tm_ffn=tm_ffn, n_experts=pstart.shape[0]),
        grid_spec=grid_spec,
        out_shape=jax.ShapeDtypeStruct((n_slots, d), h.dtype),
        compiler_params=_cparams(("arbitrary",)),
        name="moe_dispatch",
    )(pos, pstart, counts, h)


def _ffn_kernel(te_ref, nv_ref, x_ref, wg_ref, wl_ref, bg_ref, bl_ref, wd_ref, bd_ref, o_ref):
    del te_ref
    t = pl.program_id(0)

    @pl.when(t < nv_ref[0])
    def _():
        x = x_ref[...].astype(BF16)
        g = jnp.minimum(_dot(x, wg_ref[...]) + bg_ref[...], SWIGLU_LIMIT)
        l = jnp.clip(_dot(x, wl_ref[...]) + bl_ref[...], -SWIGLU_LIMIT, SWIGLU_LIMIT)
        act = g * _sigmoid(SWIGLU_ALPHA * g) * (l + 1.0)
        o_ref[...] = _dot(act.astype(BF16), wd_ref[...]) + bd_ref[...]

    @pl.when(t >= nv_ref[0])
    def _():
        o_ref[...] = jnp.zeros(o_ref.shape, o_ref.dtype)


def _expert_ffn(xg, tile_expert, n_valid, w_up, b_up, w_dn, b_dn, tm):
    p, d = xg.shape
    dff = w_up.shape[-1]
    e = w_up.shape[0]
    grid_spec = pltpu.PrefetchScalarGridSpec(
        num_scalar_prefetch=2,
        grid=(p // tm,),
        in_specs=[pl.BlockSpec((tm, d), lambda t, te, nv: (jnp.minimum(t, nv[0] - 1), 0)),
                  pl.BlockSpec((None, None, d, dff), lambda t, te, nv: (te[t], 0, 0, 0)),
                  pl.BlockSpec((None, None, d, dff), lambda t, te, nv: (te[t], 1, 0, 0)),
                  pl.BlockSpec((None, None, 1, dff), lambda t, te, nv: (te[t], 0, 0, 0)),
                  pl.BlockSpec((None, None, 1, dff), lambda t, te, nv: (te[t], 1, 0, 0)),
                  pl.BlockSpec((None, dff, d), lambda t, te, nv: (te[t], 0, 0)),
                  pl.BlockSpec((None, 1, d), lambda t, te, nv: (te[t], 0, 0))],
        out_specs=pl.BlockSpec((tm, d), lambda t, te, nv: (t, 0)))
    return pl.pallas_call(
        _ffn_kernel,
        grid_spec=grid_spec,
        out_shape=jax.ShapeDtypeStruct((p, d), F32),
        compiler_params=_cparams(("arbitrary",)),
        name="moe_expert_ffn",
    )(tile_expert, n_valid, xg, w_up, w_up, b_up.reshape(e, 2, 1, dff), b_up.reshape(e, 2, 1, dff),
      w_dn, b_dn.reshape(e, 1, d))


def _combine_kernel(pos_ref, y_hbm, tg_ref, x1_ref, gt2_ref, gpost_ref, o_ref, buf, sem, *, tm):
    base = pl.program_id(0) * tm * TOP_K

    def issue(r, carry):
        for kk in range(TOP_K):
            _row_copy(y_hbm, pos_ref[base + r * TOP_K + kk], buf.at[kk], r, sem).start()
        return carry

    lax.fori_loop(0, tm, issue, 0, unroll=4)

    def wait(r, carry):
        for kk in range(TOP_K):
            _row_copy(y_hbm, 0, buf.at[kk], r, sem).wait()
        return carry

    lax.fori_loop(0, tm, wait, 0, unroll=4)

    tg = tg_ref[...]
    f = tg[:, 0:1] * buf[0]
    for kk in range(1, TOP_K):
        f = f + tg[:, kk:kk + 1] * buf[kk]
    o_ref[...] = x1_ref[...] + gt2_ref[...] * (_rms(f, EPS) * gpost_ref[...])


def _combine(ys, pos, tg, x1, mod, gpost):
    n, d = x1.shape
    tm = mod.tm
    grid_spec = pltpu.PrefetchScalarGridSpec(
        num_scalar_prefetch=1,
        grid=(n // tm,),
        in_specs=[pl.BlockSpec(memory_space=pl.ANY),
                  pl.BlockSpec((tm, LANES), lambda i, pos: (i, 0)),
                  pl.BlockSpec((tm, d), lambda i, pos: (i, 0)),
                  _with_prefetch(mod.spec(5)),
                  pl.BlockSpec((1, d), lambda i, pos: (0, 0))],
        out_specs=pl.BlockSpec((tm, d), lambda i, pos: (i, 0)),
        scratch_shapes=[pltpu.VMEM((TOP_K, tm, d), F32), pltpu.SemaphoreType.DMA(())])
    return pl.pallas_call(
        functools.partial(_combine_kernel, tm=tm),
        grid_spec=grid_spec,
        out_shape=jax.ShapeDtypeStruct((n, d), F32),
        compiler_params=_cparams(("arbitrary",)),
        name="moe_combine",
    )(pos, ys, tg, x1, mod.arr, gpost.reshape(1, d))


def _with_prefetch(spec):
    inner = spec.index_map
    return pl.BlockSpec(spec.block_shape, lambda i, _s: inner(i))


def _route(top_i, rank, counts, tm):
    n = top_i.shape[0]
    n_experts = counts.shape[0]
    pcounts = (counts + tm - 1) // tm * tm
    pend = jnp.cumsum(pcounts)
    pstart = pend - pcounts
    experts = jnp.arange(n_experts, dtype=jnp.int32)
    pos = jnp.sum(jnp.where(top_i[:, :, None] == experts, pstart, 0), axis=-1) + rank
    n_slots = (n * TOP_K + n_experts * (tm - 1) + tm - 1) // tm * tm
    tiles = jnp.arange(n_slots // tm, dtype=jnp.int32) * tm
    tile_expert = jnp.minimum(jnp.sum((pend[None, :] <= tiles[:, None]).astype(jnp.int32), axis=1),
                              n_experts - 1)
    n_valid = (pend[-1] // tm).reshape(1)
    return pos.reshape(-1), pstart, tile_expert, n_valid, n_slots


def _moe(h2, top_i, rank, counts, tg, x1, mod, gpost, w_up, b_up, w_dn, b_dn, tm_ffn):
    pos, pstart, tile_expert, n_valid, n_slots = _route(top_i, rank, counts, tm_ffn)
    xg = _dispatch_rows(h2, pos, pstart, counts, n_slots, _pick(h2.shape[0], 512), tm_ffn)
    ys = _expert_ffn(xg, tile_expert, n_valid, w_up, b_up, w_dn, b_dn, tm_ffn)
    return _combine(ys, pos, tg, x1, mod, gpost)


def _layer(x, mod, lw, attend, retain, d):
    tm = mod.tm
    cq, ck, cv = 0, d, 2 * d
    rq, rk, rv, rg = 3 * d, 4 * d, 5 * d, 7 * d
    ga, gb = 9 * d, 10 * d
    h = _prenorm(x, lw['g_pre1'], mod, 1, 0)
    tm_mm = _pick(x.shape[0], 512)
    dq, = _matmul(h, lw['w_in'], cq, d, (BF16,), tm_mm)
    k32, k16 = _matmul(h, lw['w_in'], ck, d, (F32, BF16), tm_mm)
    v32, v16 = _matmul(h, lw['w_in'], cv, d, (F32, BF16), tm_mm)
    z, = _matmul(h, lw['w_in'], rq, 8 * d, (BF16,), tm_mm)
    o_a = attend(dq, k16, v16)
    o_b, state = retain(z, rq - rq, rk - rq, rv - rq, rg - rq)
    x1, h2, routed, tg, counts = _postmix(o_a, o_b, z, ga - rq, gb - rq, x, mod,
                                          lw['w_a'], lw['w_b'], lw['w_o'], lw['g_post1'], lw['g_pre2'],
                                          lw['w_r'], lw['b_r'])
    n_experts = lw['w_up'].shape[0]
    y = _moe(h2, routed[:, :TOP_K], routed[:, TOP_K:2 * TOP_K], counts[0, :n_experts], tg, x1, mod,
             lw['g_post2'], lw['w_up'], lw['b_up'], lw['w_dn'], lw['b_dn'], lw['tm_ffn'])
    return y, k32, v32, state


def _pick(n, pref):
    t = min(n, pref)
    while n % t:
        t //= 2
    return t


def kernel(x_prompt, x_sample, cache_k, cache_v, state_ret, page_table, c_prompt, c_sample, w_ada, b_ada, norm_pre_mix, norm_post_mix, norm_pre_ffn, norm_post_ffn, w_in, lambda_q1, lambda_k1, lambda_q2, lambda_k2, diff_subln, w_branch_a, w_branch_b, w_out, w_router, b_router, w_exp_up, b_exp_up, w_exp_down, b_exp_down):
    batch, seq, d = x_prompt.shape
    nb, dec_seq, _ = x_sample.shape
    assert dec_seq == 1
    depth = w_ada.shape[0]
    n_phys = cache_k.shape[1]
    n_pages = page_table.shape[1]
    past = n_pages * PAGE_SIZE
    n_experts = w_router.shape[-1]
    dff = w_exp_down.shape[-2]
    dk = d // RET_HEADS

    inv = ROPE_BASE ** (-jnp.arange(dk // 2, dtype=F32) * 2.0 / dk)
    ang_p = jnp.arange(seq, dtype=jnp.int32).astype(F32)[:, None] * inv[None, :]
    ang_s = jnp.full((1, 1), past, jnp.int32).astype(F32) * inv[None, :]
    cos_p, sin_p, cos_s, sin_s = jnp.cos(ang_p), jnp.sin(ang_p), jnp.cos(ang_s), jnp.sin(ang_s)
    lg = jnp.log(1.0 - 2.0 ** (-5.0 - jnp.arange(RET_HEADS, dtype=F32)))

    cache_k2 = cache_k.reshape(depth * n_phys, PAGE_SIZE * DIFF_HEADS, HEAD_W)
    cache_v2 = cache_v.reshape(depth * n_phys, PAGE_SIZE * DIFF_HEADS, HEAD_W)

    tm_p = _pick(batch * seq, 256)
    t_attn = _pick(seq, 512)
    c_ret = _pick(seq, 256)
    pages_per_step = _pick(n_pages, 8)

    yp = x_prompt.reshape(batch * seq, d)
    ys = x_sample.reshape(nb, d)
    c_all = jnp.concatenate([c_prompt, c_sample], axis=0)
    pad = (-c_all.shape[0]) % 8
    c_all = jnp.pad(c_all, ((0, pad), (0, 0)))
    outs = [[] for _ in range(6)]
    for l in range(depth):
        lam_init = 0.8 - 0.6 * math.exp(-0.3 * l)
        lams = tuple(a[l].reshape(1, DIFF_HD) for a in (lambda_q1, lambda_k1, lambda_q2, lambda_k2))
        table = _ada(c_all, w_ada[l], b_ada[l])
        mod_p = _Mod(table[:batch], d, tm_p, seq)
        mod_s = _Mod(table[batch:batch + nb], d, nb, 1)
        w_up = w_exp_up[l].reshape(n_experts, d, dff, 2).transpose(0, 3, 1, 2).astype(BF16)
        b_up = b_exp_up[l].reshape(n_experts, dff, 2).transpose(0, 2, 1)
        w_r = jnp.pad(w_router[l], ((0, 0), (0, LANES - n_experts))).astype(BF16)
        b_r = jnp.pad(b_router[l], (0, LANES - n_experts), constant_values=NEG).reshape(1, LANES)
        lw = dict(g_pre1=norm_pre_mix[l], g_post1=norm_post_mix[l], g_pre2=norm_pre_ffn[l],
                  g_post2=norm_post_ffn[l], w_in=w_in[l],
                  w_a=w_branch_a[l].astype(BF16), w_b=w_branch_b[l].astype(BF16),
                  w_o=w_out[l].astype(BF16), w_r=w_r, b_r=b_r,
                  w_up=w_up, b_up=b_up, w_dn=w_exp_down[l].astype(BF16), b_dn=b_exp_down[l])

        attend_p = lambda q, k, v: _attn_prompt(q, k, v, lams, diff_subln[l], batch, seq, lam_init, t_attn)
        retain_p = lambda z, a, b, c, g: _ret_prompt(z, a, b, c, g, batch, seq, cos_p, sin_p, lg, c_ret)
        yp, kp, vp, rp = _layer(yp, mod_p, dict(lw, tm_ffn=_pick(batch * seq, 256)), attend_p, retain_p, d)

        attend_s = lambda q, k, v: _attn_decode(q, k, v, cache_k2, cache_v2, l * n_phys, page_table,
                                                lams, diff_subln[l], lam_init, pages_per_step)
        retain_s = lambda z, a, b, c, g: _ret_decode(z, a, b, c, g, state_ret[l], cos_s, sin_s, lg)
        ys, ks, vs, rs = _layer(ys, mod_s, dict(lw, tm_ffn=16), attend_s, retain_s, d)

        for lst, val in zip(outs, (kp.reshape(batch, seq, DIFF_HEADS, HEAD_W),
                                   vp.reshape(batch, seq, DIFF_HEADS, HEAD_W),
                                   ks.reshape(nb, 1, DIFF_HEADS, HEAD_W),
                                   vs.reshape(nb, 1, DIFF_HEADS, HEAD_W), rp, rs)):
            lst.append(val)
    return (yp.reshape(batch, seq, d), ys.reshape(nb, 1, d)) + tuple(jnp.stack(o) for o in outs)
```
